```python
import jax
import jax.numpy as jnp
from jax import lax
import numpy as np

D_MODEL = 1024
BATCH = 4
SEQ = 4096
DEPTH = 2

HGRN_WIDTH = D_MODEL // 2
HGRN_HEAD_DIM = 128
HGRN_HEADS = HGRN_WIDTH // HGRN_HEAD_DIM
HGRN_CHUNK = 64
HGRN_PROJ = 4 * HGRN_WIDTH
RWKV_WIDTH = D_MODEL // 2
RWKV_HEAD_DIM = 64
RWKV_HEADS = RWKV_WIDTH // RWKV_HEAD_DIM
RWKV_DECAY_LORA = 64
RWKV_AAA_LORA = 64
RWKV_GATE_LORA = 128
RWKV_LN_EPS = 64e-5
RWKV_PROJ = 3 * RWKV_WIDTH + RWKV_DECAY_LORA + RWKV_AAA_LORA + RWKV_GATE_LORA
AR_PROJ = HGRN_PROJ + RWKV_PROJ
ATTN_HEADS = 8
ATTN_HEAD_DIM = D_MODEL // ATTN_HEADS
MOBA_BLOCK = 256
MOBA_TOPK = 3
MOBA_QCHUNK = 64
ROPE_THETA = 10000.0
D_FF = 4 * D_MODEL
PLE_DIM = 256
N_EVEN = (DEPTH + 1) // 2
N_ODD = DEPTH // 2
NORM_EPS = 1e-6

kernel_name = 'hybrid_hgrn2_rwkv7_moba_trunk'


def rms_norm(x, g, eps=NORM_EPS):
    xf = x.astype(jnp.float32)
    y = xf * lax.rsqrt(jnp.mean(xf * xf, axis=-1, keepdims=True) + eps)
    return (y * g.astype(jnp.float32)).astype(x.dtype)


def head_layer_norm(y, w, b, eps=RWKV_LN_EPS):
    yf = y.astype(jnp.float32)
    mu = jnp.mean(yf, axis=-1, keepdims=True)
    var = jnp.mean(jnp.square(yf - mu), axis=-1, keepdims=True)
    return (yf - mu) * lax.rsqrt(var + eps) * w + b


def rotary(x, pos):
    half = x.shape[-1] // 2
    inv_freq = jnp.power(ROPE_THETA, -jnp.arange(half, dtype=jnp.float32) / half)
    ang = pos.astype(jnp.float32)[:, None] * inv_freq[None, :]
    cos = jnp.cos(ang)[None, :, None, :]
    sin = jnp.sin(ang)[None, :, None, :]
    xf = x.astype(jnp.float32)
    x1, x2 = xf[..., :half], xf[..., half:]
    return jnp.concatenate([x1 * cos - x2 * sin, x2 * cos + x1 * sin], axis=-1).astype(x.dtype)


def token_shift(u):
    return jnp.pad(u[:, :-1], ((0, 0), (1, 0), (0, 0)))


def hgrn2_chunked(q, k, v, log_f):
    B, H, S, dk = q.shape
    dv = v.shape[-1]
    C = HGRN_CHUNK
    n = S // C

    def to_chunks(t):
        return t.reshape(B, H, n, C, t.shape[-1]).transpose(2, 0, 1, 3, 4)

    causal = jnp.tril(jnp.ones((C, C), dtype=bool))

    def step(state, inp):
        qc, kc, vc, gc = inp
        b = jnp.cumsum(gc, axis=2)
        o_inter = jnp.einsum('bhtk,bhkv->bhtv', qc * jnp.exp(b), state)
        diff = jnp.where(causal[None, None, :, :, None],
                         b[:, :, :, None, :] - b[:, :, None, :, :], -jnp.inf)
        scores = jnp.einsum('bhtk,bhsk,bhtsk->bhts', qc, kc, jnp.exp(diff))
        o_intra = jnp.einsum('bhts,bhsv->bhtv', scores, vc)
        b_last = b[:, :, -1:, :]
        state = (state * jnp.exp(b_last)[:, :, 0, :, None]
                 + jnp.einsum('bhsk,bhsv->bhkv', kc * jnp.exp(b_last - b), vc))
        return state, o_inter + o_intra

    state0 = jnp.zeros((B, H, dk, dv), jnp.float32)
    _, o = lax.scan(step, state0, (to_chunks(q), to_chunks(k), to_chunks(v), to_chunks(log_f)))
    return o.transpose(1, 2, 0, 3, 4).reshape(B, H, S, dv)


def rwkv7_scan(r, decay, k, v, kk, a):
    B, S, H, N = r.shape

    def step(state, inp):
        r_t, w_t, k_t, v_t, kk_t, a_t = inp
        sa = jnp.einsum('bhvk,bhk->bhv', state, -kk_t)
        state = (state * w_t[:, :, None, :]
                 + sa[..., None] * (kk_t * a_t)[:, :, None, :]
                 + v_t[..., None] * k_t[:, :, None, :])
        return state, jnp.einsum('bhvk,bhk->bhv', state, r_t)

    xs = (r.transpose(1, 0, 2, 3), decay.transpose(1, 0, 2, 3), k.transpose(1, 0, 2, 3),
          v.transpose(1, 0, 2, 3), kk.transpose(1, 0, 2, 3), a.transpose(1, 0, 2, 3))
    state0 = jnp.zeros((B, H, N, N), jnp.float32)
    _, y = lax.scan(step, state0, xs)
    return y.transpose(1, 0, 2, 3)


def hgrn_rwkv_mixer(h, w_in, w_out, lb, onorm, mu, w0, w2, a0, a2, g2, k_k, k_a, r_k, ln_w, ln_b):
    B, S, _ = h.shape
    u = (h @ w_in).astype(jnp.float32)
    hq, hf, hi, hg, ur = jnp.split(u, [HGRN_WIDTH, 2 * HGRN_WIDTH, 3 * HGRN_WIDTH, HGRN_PROJ], axis=-1)

    def heads(t, n_heads):
        return t.reshape(B, S, n_heads, -1).transpose(0, 2, 1, 3)

    f = lb + (1.0 - lb) * jax.nn.sigmoid(hf)
    o_a = hgrn2_chunked(heads(jax.nn.silu(hq), HGRN_HEADS), heads(1.0 - f, HGRN_HEADS),
                        heads(hi, HGRN_HEADS), heads(jnp.log(f), HGRN_HEADS))
    o_a = rms_norm(o_a.transpose(0, 2, 1, 3), onorm.reshape(HGRN_HEADS, HGRN_HEAD_DIM))
    o_a = o_a.reshape(B, S, HGRN_WIDTH) * jax.nn.silu(hg)

    ur = ur + (token_shift(ur) - ur) * mu
    r, k, v, wd, ad, gd = jnp.split(
        ur, [RWKV_WIDTH, 2 * RWKV_WIDTH, 3 * RWKV_WIDTH, 3 * RWKV_WIDTH + RWKV_DECAY_LORA,
             3 * RWKV_WIDTH + RWKV_DECAY_LORA + RWKV_AAA_LORA], axis=-1)
    w = -jax.nn.softplus(-(w0 + jnp.tanh(wd) @ w2)) - 0.5
    decay = jnp.exp(-jnp.exp(w))
    a = jax.nn.sigmoid(a0 + ad @ a2)
    g = jax.nn.sigmoid(gd) @ g2
    kk = (k * k_k).reshape(B, S, RWKV_HEADS, RWKV_HEAD_DIM)
    kk = kk / jnp.maximum(jnp.sqrt(jnp.sum(kk * kk, axis=-1, keepdims=True)), 1e-12)
    k = k * (1.0 + (a - 1.0) * k_a)

    def rh(t):
        return t.reshape(B, S, RWKV_HEADS, RWKV_HEAD_DIM)

    r4, k4, v4 = rh(r), rh(k), rh(v)
    y = rwkv7_scan(r4, rh(decay), k4, v4, kk, rh(a))
    y = head_layer_norm(y, ln_w.reshape(RWKV_HEADS, RWKV_HEAD_DIM),
                        ln_b.reshape(RWKV_HEADS, RWKV_HEAD_DIM))
    bonus = jnp.sum(r4 * k4 * r_k.reshape(RWKV_HEADS, RWKV_HEAD_DIM), axis=-1, keepdims=True) * v4
    o_b = (y + bonus).reshape(B, S, RWKV_WIDTH) * g

    o = jnp.concatenate([o_a, o_b], axis=-1).astype(h.dtype)
    return o @ w_out


def moba_attention(q, k, v):
    B, H, S, Dh = q.shape
    nb = -(-S // MOBA_BLOCK)
    pad = nb * MOBA_BLOCK - S
    kb = jnp.pad(k, ((0, 0), (0, 0), (0, pad), (0, 0))).reshape(B, H, nb, MOBA_BLOCK, Dh)
    vb = jnp.pad(v, ((0, 0), (0, 0), (0, pad), (0, 0))).reshape(B, H, nb, MOBA_BLOCK, Dh)
    scale = Dh ** -0.5
    q_blk = jnp.arange(S) // MOBA_BLOCK
    n_sel = min(MOBA_TOPK, nb - 1)
    if n_sel > 0:
        k_mean = jnp.mean(kb.astype(jnp.float32), axis=3)
        gate = jnp.einsum('bhsd,bhnd->bhsn', q.astype(jnp.float32), k_mean)
        past = jnp.arange(nb)[None, :] < q_blk[:, None]
        gate = jnp.where(past, gate, -jnp.inf)
        _, sel = lax.top_k(gate, n_sel)
    else:
        sel = jnp.zeros((B, H, S, 0), jnp.int32)
    valid = jnp.arange(n_sel)[None, :] < q_blk[:, None]

    qc_size = MOBA_QCHUNK
    nc = S // qc_size
    q_chunks = q.reshape(B, H, nc, qc_size, Dh).transpose(2, 0, 1, 3, 4)
    sel_chunks = sel.reshape(B, H, nc, qc_size, n_sel).transpose(2, 0, 1, 3, 4)
    valid_chunks = valid.reshape(nc, qc_size, n_sel)
    bi = jnp.arange(B)[:, None, None]
    hi = jnp.arange(H)[None, :, None]

    def attend_chunk(inp):
        c, qc, sel_c, valid_c = inp
        q_pos = c * qc_size + jnp.arange(qc_size)
        own = (c * qc_size) // MOBA_BLOCK
        k_own = lax.dynamic_index_in_dim(kb, own, axis=2, keepdims=False)
        v_own = lax.dynamic_index_in_dim(vb, own, axis=2, keepdims=False)
        k_pos = own * MOBA_BLOCK + jnp.arange(MOBA_BLOCK)
        s_own = jnp.einsum('bhqd,bhkd->bhqk', qc, k_own).astype(jnp.float32) * scale
        scores = [jnp.where(k_pos[None, :] <= q_pos[:, None], s_own, -jnp.inf)]
        for j in range(n_sel):
            k_sel = kb[bi, hi, sel_c[..., j]]
            s = jnp.einsum('bhqd,bhqkd->bhqk', qc, k_sel).astype(jnp.float32) * scale
            scores.append(jnp.where(valid_c[:, j][:, None], s, -jnp.inf))
        probs = jax.nn.softmax(jnp.concatenate(scores, axis=-1), axis=-1).astype(vb.dtype)
        out = jnp.einsum('bhqk,bhkd->bhqd', probs[..., :MOBA_BLOCK], v_own)
        for j in range(n_sel):
            v_sel = vb[bi, hi, sel_c[..., j]]
            p_j = probs[..., (j + 1) * MOBA_BLOCK:(j + 2) * MOBA_BLOCK]
            out = out + jnp.einsum('bhqk,bhqkd->bhqd', p_j, v_sel)
        return out

    o = lax.map(attend_chunk, (jnp.arange(nc), q_chunks, sel_chunks, valid_chunks))
    return o.transpose(1, 2, 0, 3, 4).reshape(B, H, S, Dh)


def moba_mixer(h, w_qkv, w_o, q_gain, k_gain, pos):
    B, S, D = h.shape
    q, k, v = jnp.split(h @ w_qkv, 3, axis=-1)
    q = q.reshape(B, S, ATTN_HEADS, ATTN_HEAD_DIM)
    k = k.reshape(B, S, ATTN_HEADS, ATTN_HEAD_DIM)
    v = v.reshape(B, S, ATTN_HEADS, ATTN_HEAD_DIM)
    q = rotary(rms_norm(q, q_gain), pos)
    k = rotary(rms_norm(k, k_gain), pos)
    o = moba_attention(q.transpose(0, 2, 1, 3), k.transpose(0, 2, 1, 3), v.transpose(0, 2, 1, 3))
    return o.transpose(0, 2, 1, 3).reshape(B, S, D) @ w_o


def setup_inputs(seed: int = 0) -> dict:
    key = jax.random.key(seed)
    ks = iter(jax.random.split(key, 40))

    def normal(shape, scale):
        return jax.random.normal(next(ks), shape, jnp.float32) * scale

    def gain(shape):
        return 1.0 + normal(shape, 0.05)

    D = D_MODEL
    mix_w = HGRN_WIDTH + RWKV_WIDTH
    return {
        'x': normal((BATCH, SEQ, D), 1.0),
        'p': normal((DEPTH, BATCH, SEQ, PLE_DIM), 1.0),
        'attn_norm': gain((DEPTH, D)),
        'mlp_norm': gain((DEPTH, D)),
        'w_in_ar': normal((N_EVEN, D, AR_PROJ), D ** -0.5),
        'w_out_ar': normal((N_EVEN, mix_w, D), mix_w ** -0.5),
        'hgrn_lb': normal((DEPTH + 1, HGRN_WIDTH), 0.5),
        'hgrn_onorm': gain((N_EVEN, HGRN_WIDTH)),
        'rwkv_mu': jax.random.uniform(next(ks), (N_EVEN, RWKV_PROJ), jnp.float32),
        'rwkv_w0': jax.random.uniform(next(ks), (N_EVEN, RWKV_WIDTH), jnp.float32, -3.0, 1.0),
        'rwkv_w2': normal((N_EVEN, RWKV_DECAY_LORA, RWKV_WIDTH), 0.1 * RWKV_DECAY_LORA ** -0.5),
        'rwkv_a0': normal((N_EVEN, RWKV_WIDTH), 0.5),
        'rwkv_a2': normal((N_EVEN, RWKV_AAA_LORA, RWKV_WIDTH), 0.1 * RWKV_AAA_LORA ** -0.5),
        'rwkv_g2': normal((N_EVEN, RWKV_GATE_LORA, RWKV_WIDTH), RWKV_GATE_LORA ** -0.5),
        'rwkv_kk': 0.85 + normal((N_EVEN, RWKV_WIDTH), 0.05),
        'rwkv_ka': 1.0 + normal((N_EVEN, RWKV_WIDTH), 0.05),
        'rwkv_rk': normal((N_EVEN, RWKV_WIDTH), 0.1),
        'rwkv_ln_w': gain((N_EVEN, RWKV_WIDTH)),
        'rwkv_ln_b': normal((N_EVEN, RWKV_WIDTH), 0.02),
        'w_qkv': normal((N_ODD, D, 3 * D), D ** -0.5),
        'w_o_attn': normal((N_ODD, D, D), D ** -0.5),
        'q_norm': gain((N_ODD, ATTN_HEAD_DIM)),
        'k_norm': gain((N_ODD, ATTN_HEAD_DIM)),
        'w_up': normal((DEPTH, D, D_FF), D ** -0.5),
        'w_down': normal((DEPTH, D_FF, D), D_FF ** -0.5),
        'ple_proj': normal((DEPTH, PLE_DIM, D), PLE_DIM ** -0.5),
        'ple_norm': gain((DEPTH, D)),
        'ple_gate': normal((DEPTH, D, D), D ** -0.5),
    }


def reference(x, p, attn_norm, mlp_norm, w_in_ar, w_out_ar, hgrn_lb, hgrn_onorm,
              rwkv_mu, rwkv_w0, rwkv_w2, rwkv_a0, rwkv_a2, rwkv_g2, rwkv_kk, rwkv_ka, rwkv_rk,
              rwkv_ln_w, rwkv_ln_b, w_qkv, w_o_attn, q_norm, k_norm, w_up, w_down,
              ple_proj, ple_norm, ple_gate):
    S = x.shape[1]
    pos = jnp.arange(S)
    lb_all = jnp.cumsum(jax.nn.softmax(hgrn_lb.astype(jnp.float32), axis=0), axis=0)
    for l in range(DEPTH):
        h = rms_norm(x, attn_norm[l])
        if l % 2 == 0:
            e = l // 2
            mix = hgrn_rwkv_mixer(h, w_in_ar[e], w_out_ar[e], lb_all[l], hgrn_onorm[e],
                                  rwkv_mu[e], rwkv_w0[e], rwkv_w2[e], rwkv_a0[e], rwkv_a2[e],
                                  rwkv_g2[e], rwkv_kk[e], rwkv_ka[e], rwkv_rk[e],
                                  rwkv_ln_w[e], rwkv_ln_b[e])
        else:
            o = l // 2
            mix = moba_mixer(h, w_qkv[o], w_o_attn[o], q_norm[o], k_norm[o], pos)
        x = x + mix.astype(x.dtype)
        h = rms_norm(x, mlp_norm[l])
        x = x + jnp.square(jax.nn.relu(h @ w_up[l])) @ w_down[l]
        ple = rms_norm(p[l] @ ple_proj[l], ple_norm[l])
        x = x + ple * jax.nn.sigmoid(x @ ple_gate[l])
    return x
```

```python
import functools
import math

import jax
import jax.numpy as jnp
from jax import lax
from jax.experimental import pallas as pl
from jax.experimental.pallas import tpu as pltpu

_F32 = jnp.float32
_BF = jnp.bfloat16

_NORM_EPS = 1e-6
_RWKV_LN_EPS = 64e-5
_ROPE_THETA = 10000.0

_HGRN_HEAD = 128
_HGRN_CHUNK = 128
_HGRN_SUB = 16
_RWKV_HEAD = 64
_RWKV_CHUNK = 64
_RWKV_BLK = 16
_MOBA_BLOCK = 256
_MOBA_TOPK = 3
_LANES = 128
_VMEM_LIMIT = 56 * 1024 * 1024

_NN = (((1,), (0,)), ((), ()))
_NT = (((1,), (1,)), ((), ()))
_TN = (((0,), (0,)), ((), ()))


def _mm(a, b, dims=_NN):
    return lax.dot_general(a, b, dims, preferred_element_type=_F32)


def _bdot(a, b, dims=_NN):
    return _mm(a.astype(_BF), b.astype(_BF), dims)


def _split2(a):
    hi = a.astype(_BF)
    lo = (a - hi.astype(_F32)).astype(_BF)
    return hi, lo


def _split3(a):
    hi = a.astype(_BF)
    r = a - hi.astype(_F32)
    mid = r.astype(_BF)
    lo = (r - mid.astype(_F32)).astype(_BF)
    return hi, mid, lo


def _dot3(a, b, dims=_NN):
    ah, al = _split2(a)
    bh, bl = _split2(b)
    return _mm(ah, bh, dims) + (_mm(ah, bl, dims) + _mm(al, bh, dims))


def _dot_exact_lhs(a_bf, b):
    h, m, l = _split3(b)
    return _mm(a_bf, h) + (_mm(a_bf, m) + _mm(a_bf, l))


def _dot_exact_rhs(a, b_bf):
    h, m, l = _split3(a)
    return _mm(h, b_bf) + (_mm(m, b_bf) + _mm(l, b_bf))


def _rmsnorm(x, g):
    return x * lax.rsqrt(jnp.mean(x * x, axis=-1, keepdims=True) + _NORM_EPS) * g


def _sigmoid(x):
    return 1.0 / (1.0 + jnp.exp(-x))


def _silu(x):
    return x * _sigmoid(x)


def _params(*sem):
    return pltpu.CompilerParams(dimension_semantics=sem, vmem_limit_bytes=_VMEM_LIMIT)


def _norm_matmul_kernel(x_ref, g_ref, w_ref, o_ref):
    h = _rmsnorm(x_ref[...], g_ref[...]).astype(_BF)
    o_ref[...] = _mm(h, w_ref[...])


def _norm_matmul(x, g, w_bf, tm=256):
    t, d = x.shape
    n = w_bf.shape[1]
    return pl.pallas_call(
        _norm_matmul_kernel,
        grid=(t // tm,),
        in_specs=[
            pl.BlockSpec((tm, d), lambda i: (i, 0)),
            pl.BlockSpec((1, d), lambda i: (0, 0)),
            pl.BlockSpec((d, n), lambda i: (0, 0)),
        ],
        out_specs=pl.BlockSpec((tm, n), lambda i: (i, 0)),
        out_shape=jax.ShapeDtypeStruct((t, n), _F32),
        compiler_params=_params("parallel"),
        name="norm_matmul",
    )(x, g.reshape(1, d), w_bf)


def _hgrn_kernel(hq_ref, hf_ref, hi_ref, hg_ref, lb_ref, on_ref, o_ref,
                 st_ref, q_s, k_s, b_s, o_s):
    ch, sub = _HGRN_CHUNK, _HGRN_SUB

    @pl.when(pl.program_id(2) == 0)
    def _():
        st_ref[...] = jnp.zeros_like(st_ref)

    lb = lb_ref[...]
    f = lb + (1.0 - lb) * _sigmoid(hf_ref[...])
    g = jnp.log(f)
    row = lax.broadcasted_iota(jnp.int32, (ch, ch), 0)
    col = lax.broadcasted_iota(jnp.int32, (ch, ch), 1)
    tri = jnp.where(row >= col, 1.0, 0.0).astype(_BF)
    b = _dot_exact_lhs(tri, g)
    q = _silu(hq_ref[...])
    k = 1.0 - f
    q_s[...] = q
    k_s[...] = k
    b_s[...] = b

    st = st_ref[...]
    o_s[...] = _bdot(q * jnp.exp(b), st, _NT)

    ones = jnp.ones((_HGRN_HEAD, _LANES), _BF)
    rsub = lax.broadcasted_iota(jnp.int32, (sub, _HGRN_HEAD), 0)
    for i in range(ch // sub):
        r0 = i * sub
        rows = slice(r0, r0 + sub)
        b_i = b_s[rows, :]
        q_i = q_s[rows, :]
        k_i = k_s[rows, :]
        o_i = o_s[rows, :]
        if i > 0:
            base = b_s[r0 - 1:r0, :]
            qh = q_i * jnp.exp(b_i - base)
            kh = k_s[0:r0, :] * jnp.exp(base - b_s[0:r0, :])
            sc = _bdot(qh, kh, _NT)
            o_i = o_i + _bdot(sc, hi_ref[0:r0, :])
        parts = []
        for s in range(sub):
            d = jnp.exp(jnp.minimum(b_i - b_s[r0 + s:r0 + s + 1, :], 0.0))
            p = q_i * d * k_s[r0 + s:r0 + s + 1, :]
            parts.append(jnp.where(rsub >= s, p, 0.0).astype(_BF))
        sb = _mm(jnp.concatenate(parts, axis=0), ones)
        for s in range(sub):
            o_i = o_i + sb[s * sub:(s + 1) * sub, :] * hi_ref[r0 + s:r0 + s + 1, :]
        o_s[rows, :] = o_i

    bl = b_s[ch - 1:ch, :]
    kd = k * jnp.exp(bl - b)
    st_ref[...] = st * jnp.exp(bl) + _bdot(hi_ref[...], kd, _TN)

    o = o_s[...]
    o = _rmsnorm(o, on_ref[...])
    o_ref[...] = o * _silu(hg_ref[...])


def _hgrn2(u, lb, onorm, batch, seq):
    t = u.shape[0]
    width = lb.shape[0]
    heads = width // _HGRN_HEAD
    ch = _HGRN_CHUNK
    nc = seq // ch

    def col(off):
        return pl.BlockSpec((ch, _HGRN_HEAD), lambda b, h, c: (b * nc + c, off * heads + h))

    vec = pl.BlockSpec((1, _HGRN_HEAD), lambda b, h, c: (0, h))
    tile = pltpu.VMEM((ch, _HGRN_HEAD), _F32)
    return pl.pallas_call(
        _hgrn_kernel,
        grid=(batch, heads, nc),
        in_specs=[col(0), col(1), col(2), col(3), vec, vec],
        out_specs=pl.BlockSpec((ch, _HGRN_HEAD), lambda b, h, c: (b * nc + c, h)),
        out_shape=jax.ShapeDtypeStruct((t, width), _F32),
        scratch_shapes=[pltpu.VMEM((_HGRN_HEAD, _HGRN_HEAD), _F32), tile, tile, tile, tile],
        compiler_params=_params("parallel", "parallel", "arbitrary"),
        name="hgrn2",
    )(u, u, u, u, lb.reshape(1, width), onorm.reshape(1, width))


def _tri_inverse(a, n, blk):
    row = lax.broadcasted_iota(jnp.int32, (n, n), 0)
    col = lax.broadcasted_iota(jnp.int32, (n, n), 1)
    eye = jnp.where(row == col, 1.0, 0.0)
    shift = int(math.log2(blk))
    same = (row >> shift) == (col >> shift)
    ad = jnp.where(same, a, 0.0)
    low = a - ad
    p = ad
    x = eye + p
    for _ in range(shift - 1):
        p = _dot3(p, p)
        x = x + _dot3(x, p)
    nmat = _dot3(x, low)
    y = eye + nmat
    p = nmat
    power = 2
    while power < (n // 2) // blk:
        p = _dot3(p, p)
        y = y + _dot3(y, p)
        power *= 2
    return _dot3(y, x)


def _rwkv_kernel(r_ref, k_ref, v_ref, t_ref, mur_ref, muk_ref, muv_ref, mut_ref,
                 w0_ref, a0_ref, kk_ref, ka_ref, rk_ref, lnw_ref, lnb_ref,
                 w2_ref, a2_ref, g2_ref, o_ref,
                 st_ref, pr_ref, pk_ref, pv_ref, pt_ref):
    c = _RWKV_CHUNK
    pairs = r_ref.shape[1] // _LANES

    @pl.when(pl.program_id(1) == 0)
    def _():
        st_ref[...] = jnp.zeros_like(st_ref)
        pr_ref[...] = jnp.zeros_like(pr_ref)
        pk_ref[...] = jnp.zeros_like(pk_ref)
        pv_ref[...] = jnp.zeros_like(pv_ref)
        pt_ref[...] = jnp.zeros_like(pt_ref)

    def shift_mix(x_ref, prev_ref, mu_ref):
        x = x_ref[...]
        first = lax.broadcasted_iota(jnp.int32, x.shape, 0) == 0
        xs = jnp.where(first, prev_ref[...], pltpu.roll(x, 1, axis=0))
        prev_ref[...] = x_ref[c - 1:c, :]
        return x + (xs - x) * mu_ref[...]

    r = shift_mix(r_ref, pr_ref, mur_ref)
    k = shift_mix(k_ref, pk_ref, muk_ref)
    v = shift_mix(v_ref, pv_ref, muv_ref)
    tl = shift_mix(t_ref, pt_ref, mut_ref)
    lo = tl[:, :_LANES]
    gd = tl[:, _LANES:]

    wpre = w0_ref[...] + _dot3(jnp.tanh(lo), w2_ref[...])
    sp = jnp.maximum(-wpre, 0.0) + jnp.log(1.0 + jnp.exp(-jnp.abs(wpre)))
    logdec = -jnp.exp(-sp - 0.5)
    a = _sigmoid(a0_ref[...] + _dot3(lo, a2_ref[...]))
    g = _dot3(_sigmoid(gd), g2_ref[...])
    kkr = k * kk_ref[...]
    k2 = k * (1.0 + (a - 1.0) * ka_ref[...])
    rkk = r * k2 * rk_ref[...]

    n = 2 * c
    row = lax.broadcasted_iota(jnp.int32, (n, n), 0)
    col = lax.broadcasted_iota(jnp.int32, (n, n), 1)
    strict = row > col
    incl = row >= col
    trow = lax.broadcasted_iota(jnp.int32, (c, c), 0)
    tcol = lax.broadcasted_iota(jnp.int32, (c, c), 1)
    tri = jnp.where(trow >= tcol, 1.0, 0.0).astype(_BF)
    lane = lax.broadcasted_iota(jnp.int32, (1, _LANES), 1)
    head0 = lane < _RWKV_HEAD
    hrow = lax.broadcasted_iota(jnp.int32, (_LANES, _LANES), 0)
    hcol = lax.broadcasted_iota(jnp.int32, (_LANES, _LANES), 1)
    same_head = (hrow < _RWKV_HEAD) == (hcol < _RWKV_HEAD)
    head_ones = jnp.where(same_head, 1.0, 0.0).astype(_BF)

    def stack(x):
        return jnp.concatenate([jnp.where(head0, x, 0.0), jnp.where(head0, 0.0, x)], axis=0)

    def twice(x):
        return jnp.concatenate([x, x], axis=0)

    def unstack(x):
        return jnp.where(head0, x[:c], x[c:])

    for p in range(pairs):
        sl = slice(p * _LANES, (p + 1) * _LANES)
        lw = logdec[:, sl]
        cum = _dot_exact_lhs(tri, lw)
        cl = cum[c - 1:c, :]
        kkp = kkr[:, sl]
        ss = _dot_exact_rhs(kkp * kkp, head_ones)
        kkp = kkp / jnp.maximum(jnp.sqrt(ss), 1e-12)
        ap = a[:, sl]
        k2p = k2[:, sl]
        rp = r[:, sl]
        vp = v[:, sl]
        ginv = jnp.exp(-cum)
        at = -kkp * jnp.exp(cum - lw)
        bt = kkp * ap * ginv
        kt = k2p * ginv
        rt = rp * jnp.exp(cum)

        at2, rt2, bt2, kt2 = stack(at), stack(rt), stack(bt), stack(kt)
        a_ab = jnp.where(strict, _bdot(at2, bt2, _NT), 0.0)
        a_ak = jnp.where(strict, _bdot(at2, kt2, _NT), 0.0)
        a_rb = jnp.where(incl, _bdot(rt2, bt2, _NT), 0.0)
        a_rk = jnp.where(incl, _bdot(rt2, kt2, _NT), 0.0)
        tinv = _tri_inverse(a_ab, n, _RWKV_BLK)

        st = st_ref[p]
        vv = twice(vp)
        rhs = twice(_bdot(at, st, _NT)) + _bdot(a_ak, vv)
        u2 = _dot3(tinv, rhs)
        uu = unstack(u2)
        y2 = twice(_bdot(rt, st, _NT)) + _bdot(a_rb, twice(uu)) + _bdot(a_rk, vv)
        y = unstack(y2)

        dl = jnp.exp(cl - cum)
        st_new = (st * jnp.exp(cl) + _bdot(uu, kkp * ap * dl, _TN) + _bdot(vp, k2p * dl, _TN))
        st_ref[p] = jnp.where(same_head, st_new, 0.0)

        inv_n = 1.0 / _RWKV_HEAD
        mu = _dot_exact_rhs(y, head_ones) * inv_n
        yc = y - mu
        var = _dot_exact_rhs(yc * yc, head_ones) * inv_n
        yn = yc * lax.rsqrt(var + _RWKV_LN_EPS) * lnw_ref[:, sl] + lnb_ref[:, sl]
        bonus = _dot_exact_rhs(rkk[:, sl], head_ones) * vp
        o_ref[:, sl] = (yn + bonus) * g[:, sl]


def _rwkv7(u, col0, mu, w0, w2, a0, a2, g2, k_k, k_a, r_k, ln_w, ln_b, batch, seq):
    t = u.shape[0]
    width = w0.shape[0]
    c = _RWKV_CHUNK
    nc = seq // c
    tail = mu.shape[0] - 3 * width
    d_lora = w2.shape[0]
    a_lora = a2.shape[0]
    assert d_lora + a_lora == _LANES and tail - _LANES == g2.shape[0]
    assert col0 % width == 0 and (col0 + 3 * width) % tail == 0
    cb = col0 // width
    w2p = jnp.concatenate([w2, jnp.zeros((a_lora, width), _F32)], axis=0)
    a2p = jnp.concatenate([jnp.zeros((d_lora, width), _F32), a2], axis=0)

    def colblk(j):
        return pl.BlockSpec((c, width), lambda b, i: (b * nc + i, cb + j))

    def vec(nn):
        return pl.BlockSpec((1, nn), lambda b, i: (0, 0))

    def mat(m):
        return pl.BlockSpec(m.shape, lambda b, i: (0, 0))

    row = lambda x: x.reshape(1, -1)
    return pl.pallas_call(
        _rwkv_kernel,
        grid=(batch, nc),
        in_specs=[colblk(0), colblk(1), colblk(2),
                  pl.BlockSpec((c, tail), lambda b, i: (b * nc + i, (col0 + 3 * width) // tail)),
                  vec(width), vec(width), vec(width), vec(tail)]
        + [vec(width)] * 7 + [mat(w2p), mat(a2p), mat(g2)],
        out_specs=pl.BlockSpec((c, width), lambda b, i: (b * nc + i, 0)),
        out_shape=jax.ShapeDtypeStruct((t, width), _F32),
        scratch_shapes=[pltpu.VMEM((width // _LANES, _LANES, _LANES), _F32),
                        pltpu.VMEM((1, width), _F32), pltpu.VMEM((1, width), _F32),
                        pltpu.VMEM((1, width), _F32), pltpu.VMEM((1, tail), _F32)],
        compiler_params=_params("parallel", "arbitrary"),
        name="rwkv7",
    )(u, u, u, u,
      row(mu[:width]), row(mu[width:2 * width]), row(mu[2 * width:3 * width]), row(mu[3 * width:]),
      row(w0), row(a0), row(k_k), row(k_a), row(r_k), row(ln_w), row(ln_b), w2p, a2p, g2)


def _post_kernel(*refs, n_mix):
    x_ref = refs[0]
    o_refs = refs[1:1 + n_mix]
    wo_refs = refs[1 + n_mix:1 + 2 * n_mix]
    (g_ref, wup_ref, wdn_ref, p_ref, wp_ref, pg_ref, wg_ref, out_ref,
     x1_s, h_s, acc_s) = refs[1 + 2 * n_mix:]
    kf = pl.program_id(1)

    @pl.when(kf == 0)
    def _():
        x1 = x_ref[...]
        for o_ref, wo_ref in zip(o_refs, wo_refs):
            x1 = x1 + _mm(o_ref[...].astype(_BF), wo_ref[...])
        x1_s[...] = x1
        h_s[...] = _rmsnorm(x1, g_ref[...]).astype(_BF)
        acc_s[...] = jnp.zeros_like(acc_s)

    act = jnp.square(jnp.maximum(_mm(h_s[...], wup_ref[...]), 0.0))
    acc_s[...] += _mm(act.astype(_BF), wdn_ref[...])

    @pl.when(kf == pl.num_programs(1) - 1)
    def _():
        x2 = x1_s[...] + acc_s[...]
        ple = _rmsnorm(_mm(p_ref[...].astype(_BF), wp_ref[...]), pg_ref[...])
        gate = _sigmoid(_mm(x2.astype(_BF), wg_ref[...]))
        out_ref[...] = x2 + ple * gate


def _post_mixer(x, mix, w_out_parts, g, w_up, w_down, p, w_p, p_g, w_g, tm=512, tf=512):
    t, d = x.shape
    dff = w_up.shape[1]
    n_mix = len(mix)
    tok = lambda w: pl.BlockSpec((tm, w), lambda i, k: (i, 0))
    full = lambda m: pl.BlockSpec(m.shape, lambda i, k: (0, 0))
    in_specs = ([tok(d)] + [tok(o.shape[1]) for o in mix] + [full(w) for w in w_out_parts]
                + [pl.BlockSpec((1, d), lambda i, k: (0, 0)),
                   pl.BlockSpec((d, tf), lambda i, k: (0, k)),
                   pl.BlockSpec((tf, d), lambda i, k: (k, 0)),
                   tok(p.shape[1]), full(w_p),
                   pl.BlockSpec((1, d), lambda i, k: (0, 0)), full(w_g)])
    return pl.pallas_call(
        functools.partial(_post_kernel, n_mix=n_mix),
        grid=(t // tm, dff // tf),
        in_specs=in_specs,
        out_specs=pl.BlockSpec((tm, d), lambda i, k: (i, 0)),
        out_shape=jax.ShapeDtypeStruct((t, d), _F32),
        scratch_shapes=[pltpu.VMEM((tm, d), _F32), pltpu.VMEM((tm, d), _BF),
                        pltpu.VMEM((tm, d), _F32)],
        compiler_params=_params("parallel", "arbitrary"),
        name="post_mixer",
    )(x, *mix, *w_out_parts, g.reshape(1, d), w_up, w_down, p, w_p, p_g.reshape(1, d), w_g)


def _qk_prep_kernel(qkv_ref, cos_ref, sin_ref, qg_ref, kg_ref, q_ref, k_ref, v_ref, km_ref):
    d = q_ref.shape[1]
    heads = d // _LANES
    j = pl.program_id(1)
    cos = cos_ref[...]
    sin = sin_ref[...]

    def rope(x, gain):
        xn = _rmsnorm(x, gain)
        return xn * cos + pltpu.roll(xn, _LANES // 2, axis=1) * sin

    for h in range(heads):
        sl = slice(h * _LANES, (h + 1) * _LANES)
        q_ref[:, sl] = rope(qkv_ref[:, sl], qg_ref[...])
        kr = rope(qkv_ref[:, d + h * _LANES:d + (h + 1) * _LANES], kg_ref[...])
        k_ref[:, sl] = kr.astype(_BF)
        km_ref[0, pl.ds(j, 1), :, sl] = jnp.mean(kr, axis=0, keepdims=True)[None]
    v_ref[...] = qkv_ref[:, 2 * d:].astype(_BF)


def _qk_prep(qkv, q_gain, k_gain, batch, seq):
    t = qkv.shape[0]
    d = qkv.shape[1] // 3
    blk = _MOBA_BLOCK
    nb = seq // blk
    half = _LANES // 2
    inv_freq = jnp.power(_ROPE_THETA, -jnp.arange(half, dtype=_F32) / half)
    ang = jnp.arange(seq, dtype=_F32)[:, None] * inv_freq[None, :]
    cos = jnp.concatenate([jnp.cos(ang), jnp.cos(ang)], axis=1)
    sin = jnp.concatenate([-jnp.sin(ang), jnp.sin(ang)], axis=1)
    tok = lambda w: pl.BlockSpec((blk, w), lambda b, j: (b * nb + j, 0))
    tab = pl.BlockSpec((blk, _LANES), lambda b, j: (j, 0))
    vec = pl.BlockSpec((1, _LANES), lambda b, j: (0, 0))
    return pl.pallas_call(
        _qk_prep_kernel,
        grid=(batch, nb),
        in_specs=[tok(3 * d), tab, tab, vec, vec],
        out_specs=[tok(d), tok(d), tok(d), pl.BlockSpec((1, nb, 1, d), lambda b, j: (b, 0, 0, 0))],
        out_shape=[jax.ShapeDtypeStruct((t, d), _F32), jax.ShapeDtypeStruct((t, d), _BF),
                   jax.ShapeDtypeStruct((t, d), _BF), jax.ShapeDtypeStruct((batch, nb, 1, d), _F32)],
        compiler_params=_params("parallel", "arbitrary"),
        name="qk_prep",
    )(qkv, cos, sin, q_gain.reshape(1, _LANES), k_gain.reshape(1, _LANES))


def _moba_kernel(q_ref, k_ref, v_ref, km_ref, o_ref):
    blk = _MOBA_BLOCK
    i = pl.program_id(2)
    q = q_ref[...]
    km = km_ref[0]
    nb = km.shape[0]
    gate = _dot3(q, km, _NT)
    bidx = lax.broadcasted_iota(jnp.int32, (blk, nb), 1)
    neg = -jnp.inf
    gate = jnp.where(bidx < i, gate, neg)
    sel = jnp.zeros((blk, nb), _F32)
    for _ in range(_MOBA_TOPK):
        m = jnp.max(gate, axis=1, keepdims=True)
        cand = jnp.where((gate == m) & (m > neg), bidx, nb)
        pick = bidx == jnp.min(cand, axis=1, keepdims=True)
        sel = jnp.where(pick, 1.0, sel)
        gate = jnp.where(pick, neg, gate)

    qs = (q * (q.shape[1] ** -0.5)).astype(_BF)

    def attend(j, keep, carry):
        m_old, l_old, acc = carry
        start = pl.multiple_of(j * blk, blk)
        s = _mm(qs, k_ref[pl.ds(start, blk), :], _NT)
        s = jnp.where(keep, s, neg)
        m_new = jnp.maximum(m_old, jnp.max(s, axis=1, keepdims=True))
        alpha = jnp.exp(m_old - m_new)
        pr = jnp.exp(s - m_new)
        l_new = alpha * l_old + jnp.sum(pr, axis=1, keepdims=True)
        acc = alpha * acc + _mm(pr.astype(_BF), v_ref[pl.ds(start, blk), :])
        return m_new, l_new, acc

    qi = lax.broadcasted_iota(jnp.int32, (blk, blk), 0)
    ki = lax.broadcasted_iota(jnp.int32, (blk, blk), 1)
    init = (jnp.full((blk, 1), neg, _F32), jnp.zeros((blk, 1), _F32),
            jnp.zeros((blk, q.shape[1]), _F32))
    carry = attend(i, ki <= qi, init)

    def past(j, carry):
        keep = jnp.sum(jnp.where(bidx == j, sel, 0.0), axis=1, keepdims=True) > 0.5
        return attend(j, keep, carry)

    _, l_fin, acc = lax.fori_loop(0, i, past, carry)
    o_ref[...] = acc / l_fin


def _moba(q, k, v, kmean, batch, seq):
    t, d = q.shape
    heads = d // _LANES
    blk = _MOBA_BLOCK
    nb = seq // blk
    qspec = pl.BlockSpec((blk, _LANES), lambda b, h, i: (b * nb + i, h))
    kvspec = pl.BlockSpec((seq, _LANES), lambda b, h, i: (b, h))
    return pl.pallas_call(
        _moba_kernel,
        grid=(batch, heads, nb),
        in_specs=[qspec, kvspec, kvspec, pl.BlockSpec((1, nb, _LANES), lambda b, h, i: (b, 0, h))],
        out_specs=qspec,
        out_shape=jax.ShapeDtypeStruct((t, d), _F32),
        compiler_params=_params("parallel", "parallel", "arbitrary"),
        name="moba_attention",
    )(q, k, v, kmean)


def kernel(x, p, attn_norm, mlp_norm, w_in_ar, w_out_ar, hgrn_lb, hgrn_onorm, rwkv_mu, rwkv_w0, rwkv_w2, rwkv_a0, rwkv_a2, rwkv_g2, rwkv_kk, rwkv_ka, rwkv_rk, rwkv_ln_w, rwkv_ln_b, w_qkv, w_o_attn, q_norm, k_norm, w_up, w_down, ple_proj, ple_norm, ple_gate):
    batch, seq, d = x.shape
    depth = p.shape[0]
    t = batch * seq
    hw = hgrn_onorm.shape[1]
    bf = lambda w: w.astype(_BF)
    lb_all = jnp.cumsum(jax.nn.softmax(hgrn_lb.astype(_F32), axis=0), axis=0)
    xt = x.reshape(t, d)
    for l in range(depth):
        if l % 2 == 0:
            e = l // 2
            u = _norm_matmul(xt, attn_norm[l], bf(w_in_ar[e]))
            o_a = _hgrn2(u, lb_all[l], hgrn_onorm[e], batch, seq)
            o_b = _rwkv7(u, 4 * hw, rwkv_mu[e], rwkv_w0[e], rwkv_w2[e], rwkv_a0[e], rwkv_a2[e],
                         rwkv_g2[e], rwkv_kk[e], rwkv_ka[e], rwkv_rk[e], rwkv_ln_w[e],
                         rwkv_ln_b[e], batch, seq)
            w_o = bf(w_out_ar[e])
            mix, w_parts = [o_a, o_b], [w_o[:hw], w_o[hw:]]
        else:
            o = l // 2
            qkv = _norm_matmul(xt, attn_norm[l], bf(w_qkv[o]))
            q, k, v, kmean = _qk_prep(qkv, q_norm[o], k_norm[o], batch, seq)
            kmean = kmean.reshape(batch, -1, d)
            mix, w_parts = [_moba(q, k, v, kmean, batch, seq)], [bf(w_o_attn[o])]
        xt = _post_mixer(xt, mix, w_parts, mlp_norm[l], bf(w_up[l]), bf(w_down[l]),
                         p[l].reshape(t, -1), bf(ple_proj[l]), ple_norm[l], bf(ple_gate[l]))
    return xt.reshape(batch, seq, d)
```

```python
import functools
import math

import jax
import jax.numpy as jnp
from jax import lax
from jax.experimental import pallas as pl
from jax.experimental.pallas import tpu as pltpu

_F32 = jnp.float32
_BF = jnp.bfloat16

_NORM_EPS = 1e-6
_RWKV_LN_EPS = 64e-5
_ROPE_THETA = 10000.0

_HGRN_HEAD = 128
_HGRN_CHUNK = 128
_HGRN_SUB = 16
_RWKV_HEAD = 64
_RWKV_CHUNK = 64
_RWKV_BLK = 16
_MOBA_BLOCK = 256
_MOBA_TOPK = 3
_MOBA_HEADS_PER_STEP = 4
_LANES = 128
_VMEM_LIMIT = 56 * 1024 * 1024

_NN = (((1,), (0,)), ((), ()))
_NT = (((1,), (1,)), ((), ()))
_TN = (((0,), (0,)), ((), ()))


def _mm(a, b, dims=_NN):
    return lax.dot_general(a, b, dims, preferred_element_type=_F32)


def _bdot(a, b, dims=_NN):
    return _mm(a.astype(_BF), b.astype(_BF), dims)


def _split2(a):
    hi = a.astype(_BF)
    lo = (a - hi.astype(_F32)).astype(_BF)
    return hi, lo


def _split3(a):
    hi = a.astype(_BF)
    r = a - hi.astype(_F32)
    mid = r.astype(_BF)
    lo = (r - mid.astype(_F32)).astype(_BF)
    return hi, mid, lo


def _dot3(a, b, dims=_NN):
    ah, al = _split2(a)
    bh, bl = _split2(b)
    return _mm(ah, bh, dims) + (_mm(ah, bl, dims) + _mm(al, bh, dims))


def _dot_exact_lhs(a_bf, b):
    h, m, l = _split3(b)
    return _mm(a_bf, h) + (_mm(a_bf, m) + _mm(a_bf, l))


def _dot_exact_rhs(a, b_bf):
    h, m, l = _split3(a)
    return _mm(h, b_bf) + (_mm(m, b_bf) + _mm(l, b_bf))


def _rmsnorm(x, g):
    return x * lax.rsqrt(jnp.mean(x * x, axis=-1, keepdims=True) + _NORM_EPS) * g


def _sigmoid(x):
    return 1.0 / (1.0 + jnp.exp(-x))


def _silu(x):
    return x * _sigmoid(x)


def _params(*sem):
    return pltpu.CompilerParams(dimension_semantics=sem, vmem_limit_bytes=_VMEM_LIMIT)


def _norm_matmul_kernel(x_ref, g_ref, w_ref, o_ref):
    h = _rmsnorm(x_ref[...], g_ref[...]).astype(_BF)
    o_ref[...] = _mm(h, w_ref[...])


def _norm_matmul(x, g, w_bf, tm=256):
    t, d = x.shape
    n = w_bf.shape[1]
    return pl.pallas_call(
        _norm_matmul_kernel,
        grid=(t // tm,),
        in_specs=[
            pl.BlockSpec((tm, d), lambda i: (i, 0)),
            pl.BlockSpec((1, d), lambda i: (0, 0)),
            pl.BlockSpec((d, n), lambda i: (0, 0)),
        ],
        out_specs=pl.BlockSpec((tm, n), lambda i: (i, 0)),
        out_shape=jax.ShapeDtypeStruct((t, n), _F32),
        compiler_params=_params("parallel"),
        name="norm_matmul",
    )(x, g.reshape(1, d), w_bf)


def _hgrn_kernel(hq_ref, hf_ref, hi_ref, hg_ref, lb_ref, on_ref, o_ref,
                 st_ref, q_s, k_s, b_s, o_s):
    ch, sub = _HGRN_CHUNK, _HGRN_SUB

    @pl.when(pl.program_id(2) == 0)
    def _():
        st_ref[...] = jnp.zeros_like(st_ref)

    lb = lb_ref[...]
    f = lb + (1.0 - lb) * _sigmoid(hf_ref[...])
    g = jnp.log(f)
    row = lax.broadcasted_iota(jnp.int32, (ch, ch), 0)
    col = lax.broadcasted_iota(jnp.int32, (ch, ch), 1)
    tri = jnp.where(row >= col, 1.0, 0.0).astype(_BF)
    b = _dot_exact_lhs(tri, g)
    q = _silu(hq_ref[...])
    k = 1.0 - f
    q_s[...] = q
    k_s[...] = k
    b_s[...] = b

    st = st_ref[...]
    o_s[...] = _bdot(q * jnp.exp(b), st, _NT)

    ones = jnp.ones((_HGRN_HEAD, _LANES), _BF)
    rsub = lax.broadcasted_iota(jnp.int32, (sub, _HGRN_HEAD), 0)
    for i in range(ch // sub):
        r0 = i * sub
        rows = slice(r0, r0 + sub)
        b_i = b_s[rows, :]
        q_i = q_s[rows, :]
        k_i = k_s[rows, :]
        o_i = o_s[rows, :]
        if i > 0:
            base = b_s[r0 - 1:r0, :]
            qh = q_i * jnp.exp(b_i - base)
            kh = k_s[0:r0, :] * jnp.exp(base - b_s[0:r0, :])
            sc = _bdot(qh, kh, _NT)
            o_i = o_i + _bdot(sc, hi_ref[0:r0, :])
        parts = []
        for s in range(sub):
            d = jnp.exp(jnp.minimum(b_i - b_s[r0 + s:r0 + s + 1, :], 0.0))
            p = q_i * d * k_s[r0 + s:r0 + s + 1, :]
            parts.append(jnp.where(rsub >= s, p, 0.0).astype(_BF))
        sb = _mm(jnp.concatenate(parts, axis=0), ones)
        for s in range(sub):
            o_i = o_i + sb[s * sub:(s + 1) * sub, :] * hi_ref[r0 + s:r0 + s + 1, :]
        o_s[rows, :] = o_i

    bl = b_s[ch - 1:ch, :]
    kd = k * jnp.exp(bl - b)
    st_ref[...] = st * jnp.exp(bl) + _bdot(hi_ref[...], kd, _TN)

    o = o_s[...]
    o = _rmsnorm(o, on_ref[...])
    o_ref[...] = o * _silu(hg_ref[...])


def _hgrn2(u, lb, onorm, batch, seq):
    t = u.shape[0]
    width = lb.shape[0]
    heads = width // _HGRN_HEAD
    ch = _HGRN_CHUNK
    nc = seq // ch

    def col(off):
        return pl.BlockSpec((ch, _HGRN_HEAD), lambda b, h, c: (b * nc + c, off * heads + h))

    vec = pl.BlockSpec((1, _HGRN_HEAD), lambda b, h, c: (0, h))
    tile = pltpu.VMEM((ch, _HGRN_HEAD), _F32)
    return pl.pallas_call(
        _hgrn_kernel,
        grid=(batch, heads, nc),
        in_specs=[col(0), col(1), col(2), col(3), vec, vec],
        out_specs=pl.BlockSpec((ch, _HGRN_HEAD), lambda b, h, c: (b * nc + c, h)),
        out_shape=jax.ShapeDtypeStruct((t, width), _F32),
        scratch_shapes=[pltpu.VMEM((_HGRN_HEAD, _HGRN_HEAD), _F32), tile, tile, tile, tile],
        compiler_params=_params("parallel", "parallel", "arbitrary"),
        name="hgrn2",
    )(u, u, u, u, lb.reshape(1, width), onorm.reshape(1, width))


def _tri_solve(a_list, rhs_list, n, blk):
    row = lax.broadcasted_iota(jnp.int32, (n, n), 0)
    col = lax.broadcasted_iota(jnp.int32, (n, n), 1)
    eye = jnp.where(row == col, 1.0, 0.0)
    shift = int(math.log2(blk))
    same = (row >> shift) == (col >> shift)
    ad = [jnp.where(same, a, 0.0) for a in a_list]
    low = [a - d for a, d in zip(a_list, ad)]
    p = ad
    x = [eye + d for d in ad]
    for _ in range(shift - 1):
        p = [_dot3(q, q) for q in p]
        x = [xi + _dot3(xi, q) for xi, q in zip(x, p)]
    nmat = [_dot3(xi, lo) for xi, lo in zip(x, low)]
    term = [_dot3(xi, r) for xi, r in zip(x, rhs_list)]
    u = term
    for _ in range((n // 2) // blk - 1):
        term = [_dot3(nm, t) for nm, t in zip(nmat, term)]
        u = [ui + t for ui, t in zip(u, term)]
    return u


def _rwkv_kernel(r_ref, k_ref, v_ref, t_ref, mur_ref, muk_ref, muv_ref, mut_ref,
                 w0_ref, a0_ref, kk_ref, ka_ref, rk_ref, lnw_ref, lnb_ref,
                 w2_ref, a2_ref, g2_ref, o_ref,
                 st_ref, pr_ref, pk_ref, pv_ref, pt_ref):
    c = _RWKV_CHUNK
    pairs = r_ref.shape[1] // _LANES

    @pl.when(pl.program_id(1) == 0)
    def _():
        st_ref[...] = jnp.zeros_like(st_ref)
        pr_ref[...] = jnp.zeros_like(pr_ref)
        pk_ref[...] = jnp.zeros_like(pk_ref)
        pv_ref[...] = jnp.zeros_like(pv_ref)
        pt_ref[...] = jnp.zeros_like(pt_ref)

    def shift_mix(x_ref, prev_ref, mu_ref):
        x = x_ref[...]
        first = lax.broadcasted_iota(jnp.int32, x.shape, 0) == 0
        xs = jnp.where(first, prev_ref[...], pltpu.roll(x, 1, axis=0))
        prev_ref[...] = x_ref[c - 1:c, :]
        return x + (xs - x) * mu_ref[...]

    r = shift_mix(r_ref, pr_ref, mur_ref)
    k = shift_mix(k_ref, pk_ref, muk_ref)
    v = shift_mix(v_ref, pv_ref, muv_ref)
    tl = shift_mix(t_ref, pt_ref, mut_ref)
    lo = tl[:, :_LANES]
    gd = tl[:, _LANES:]

    wpre = w0_ref[...] + _dot3(jnp.tanh(lo), w2_ref[...])
    sp = jnp.maximum(-wpre, 0.0) + jnp.log(1.0 + jnp.exp(-jnp.abs(wpre)))
    logdec = -jnp.exp(-sp - 0.5)
    a = _sigmoid(a0_ref[...] + _dot3(lo, a2_ref[...]))
    g = _dot3(_sigmoid(gd), g2_ref[...])
    kkr = k * kk_ref[...]
    k2 = k * (1.0 + (a - 1.0) * ka_ref[...])
    rkk = r * k2 * rk_ref[...]

    n = 2 * c
    row = lax.broadcasted_iota(jnp.int32, (n, n), 0)
    col = lax.broadcasted_iota(jnp.int32, (n, n), 1)
    strict = row > col
    incl = row >= col
    trow = lax.broadcasted_iota(jnp.int32, (c, c), 0)
    tcol = lax.broadcasted_iota(jnp.int32, (c, c), 1)
    tri = jnp.where(trow >= tcol, 1.0, 0.0).astype(_BF)
    lane = lax.broadcasted_iota(jnp.int32, (1, _LANES), 1)
    head0 = lane < _RWKV_HEAD
    hrow = lax.broadcasted_iota(jnp.int32, (_LANES, _LANES), 0)
    hcol = lax.broadcasted_iota(jnp.int32, (_LANES, _LANES), 1)
    same_head = (hrow < _RWKV_HEAD) == (hcol < _RWKV_HEAD)
    head_ones = jnp.where(same_head, 1.0, 0.0).astype(_BF)

    def stack(x):
        return jnp.concatenate([jnp.where(head0, x, 0.0), jnp.where(head0, 0.0, x)], axis=0)

    def twice(x):
        return jnp.concatenate([x, x], axis=0)

    def unstack(x):
        return jnp.where(head0, x[:c], x[c:])

    ps = range(pairs)
    sls = [slice(p * _LANES, (p + 1) * _LANES) for p in ps]
    lw = [logdec[:, sl] for sl in sls]
    cum = [_dot_exact_lhs(tri, x) for x in lw]
    kk0 = [kkr[:, sl] for sl in sls]
    ss = [_dot_exact_rhs(x * x, head_ones) for x in kk0]
    kkp = [x / jnp.maximum(jnp.sqrt(q), 1e-12) for x, q in zip(kk0, ss)]
    ap = [a[:, sl] for sl in sls]
    k2p = [k2[:, sl] for sl in sls]
    rp = [r[:, sl] for sl in sls]
    vp = [v[:, sl] for sl in sls]
    vv = [twice(x) for x in vp]
    ginv = [jnp.exp(-x) for x in cum]
    at = [-kkp[p] * jnp.exp(cum[p] - lw[p]) for p in ps]
    bt = [kkp[p] * ap[p] * ginv[p] for p in ps]
    kt = [k2p[p] * ginv[p] for p in ps]
    rt = [rp[p] * jnp.exp(cum[p]) for p in ps]
    at2 = [stack(x) for x in at]
    rt2 = [stack(x) for x in rt]
    bt2 = [stack(x) for x in bt]
    kt2 = [stack(x) for x in kt]
    a_ab = [jnp.where(strict, _bdot(at2[p], bt2[p], _NT), 0.0) for p in ps]
    a_ak = [jnp.where(strict, _bdot(at2[p], kt2[p], _NT), 0.0) for p in ps]
    a_rb = [jnp.where(incl, _bdot(rt2[p], bt2[p], _NT), 0.0) for p in ps]
    a_rk = [jnp.where(incl, _bdot(rt2[p], kt2[p], _NT), 0.0) for p in ps]

    st = [st_ref[p] for p in ps]
    rhs = [twice(_bdot(at[p], st[p], _NT)) + _bdot(a_ak[p], vv[p]) for p in ps]
    u2 = _tri_solve(a_ab, rhs, n, _RWKV_BLK)
    uu = [unstack(x) for x in u2]
    y2 = [twice(_bdot(rt[p], st[p], _NT)) + _bdot(a_rb[p], twice(uu[p])) + _bdot(a_rk[p], vv[p])
          for p in ps]
    y = [unstack(x) for x in y2]

    cl = [x[c - 1:c, :] for x in cum]
    dl = [jnp.exp(cl[p] - cum[p]) for p in ps]
    for p in ps:
        st_new = (st[p] * jnp.exp(cl[p]) + _bdot(uu[p], kkp[p] * ap[p] * dl[p], _TN)
                  + _bdot(vp[p], k2p[p] * dl[p], _TN))
        st_ref[p] = jnp.where(same_head, st_new, 0.0)

    inv_n = 1.0 / _RWKV_HEAD
    mu = [_dot_exact_rhs(x, head_ones) * inv_n for x in y]
    yc = [y[p] - mu[p] for p in ps]
    var = [_dot_exact_rhs(x * x, head_ones) * inv_n for x in yc]
    bonus = [_dot_exact_rhs(rkk[:, sl], head_ones) for sl in sls]
    for p in ps:
        sl = sls[p]
        yn = yc[p] * lax.rsqrt(var[p] + _RWKV_LN_EPS) * lnw_ref[:, sl] + lnb_ref[:, sl]
        o_ref[:, sl] = (yn + bonus[p] * vp[p]) * g[:, sl]


def _rwkv7(u, col0, mu, w0, w2, a0, a2, g2, k_k, k_a, r_k, ln_w, ln_b, batch, seq):
    t = u.shape[0]
    width = w0.shape[0]
    c = _RWKV_CHUNK
    nc = seq // c
    tail = mu.shape[0] - 3 * width
    d_lora = w2.shape[0]
    a_lora = a2.shape[0]
    assert d_lora + a_lora == _LANES and tail - _LANES == g2.shape[0]
    assert col0 % width == 0 and (col0 + 3 * width) % tail == 0
    cb = col0 // width
    w2p = jnp.concatenate([w2, jnp.zeros((a_lora, width), _F32)], axis=0)
    a2p = jnp.concatenate([jnp.zeros((d_lora, width), _F32), a2], axis=0)

    def colblk(j):
        return pl.BlockSpec((c, width), lambda b, i: (b * nc + i, cb + j))

    def vec(nn):
        return pl.BlockSpec((1, nn), lambda b, i: (0, 0))

    def mat(m):
        return pl.BlockSpec(m.shape, lambda b, i: (0, 0))

    row = lambda x: x.reshape(1, -1)
    return pl.pallas_call(
        _rwkv_kernel,
        grid=(batch, nc),
        in_specs=[colblk(0), colblk(1), colblk(2),
                  pl.BlockSpec((c, tail), lambda b, i: (b * nc + i, (col0 + 3 * width) // tail)),
                  vec(width), vec(width), vec(width), vec(tail)]
        + [vec(width)] * 7 + [mat(w2p), mat(a2p), mat(g2)],
        out_specs=pl.BlockSpec((c, width), lambda b, i: (b * nc + i, 0)),
        out_shape=jax.ShapeDtypeStruct((t, width), _F32),
        scratch_shapes=[pltpu.VMEM((width // _LANES, _LANES, _LANES), _F32),
                        pltpu.VMEM((1, width), _F32), pltpu.VMEM((1, width), _F32),
                        pltpu.VMEM((1, width), _F32), pltpu.VMEM((1, tail), _F32)],
        compiler_params=_params("parallel", "arbitrary"),
        name="rwkv7",
    )(u, u, u, u,
      row(mu[:width]), row(mu[width:2 * width]), row(mu[2 * width:3 * width]), row(mu[3 * width:]),
      row(w0), row(a0), row(k_k), row(k_a), row(r_k), row(ln_w), row(ln_b), w2p, a2p, g2)


def _post_kernel(*refs, n_mix):
    x_ref = refs[0]
    o_refs = refs[1:1 + n_mix]
    wo_refs = refs[1 + n_mix:1 + 2 * n_mix]
    (g_ref, wup_ref, wdn_ref, p_ref, wp_ref, pg_ref, wg_ref, out_ref,
     x1_s, h_s, acc_s) = refs[1 + 2 * n_mix:]
    kf = pl.program_id(1)

    @pl.when(kf == 0)
    def _():
        x1 = x_ref[...]
        for o_ref, wo_ref in zip(o_refs, wo_refs):
            x1 = x1 + _mm(o_ref[...].astype(_BF), wo_ref[...])
        x1_s[...] = x1
        h_s[...] = _rmsnorm(x1, g_ref[...]).astype(_BF)
        acc_s[...] = jnp.zeros_like(acc_s)

    act = jnp.square(jnp.maximum(_mm(h_s[...], wup_ref[...]), 0.0))
    acc_s[...] += _mm(act.astype(_BF), wdn_ref[...])

    @pl.when(kf == pl.num_programs(1) - 1)
    def _():
        x2 = x1_s[...] + acc_s[...]
        ple = _rmsnorm(_mm(p_ref[...].astype(_BF), wp_ref[...]), pg_ref[...])
        gate = _sigmoid(_mm(x2.astype(_BF), wg_ref[...]))
        out_ref[...] = x2 + ple * gate


def _post_mixer(x, mix, w_out_parts, g, w_up, w_down, p, w_p, p_g, w_g, tm=512, tf=512):
    t, d = x.shape
    dff = w_up.shape[1]
    n_mix = len(mix)
    tok = lambda w: pl.BlockSpec((tm, w), lambda i, k: (i, 0))
    full = lambda m: pl.BlockSpec(m.shape, lambda i, k: (0, 0))
    in_specs = ([tok(d)] + [tok(o.shape[1]) for o in mix] + [full(w) for w in w_out_parts]
                + [pl.BlockSpec((1, d), lambda i, k: (0, 0)),
                   pl.BlockSpec((d, tf), lambda i, k: (0, k)),
                   pl.BlockSpec((tf, d), lambda i, k: (k, 0)),
                   tok(p.shape[1]), full(w_p),
                   pl.BlockSpec((1, d), lambda i, k: (0, 0)), full(w_g)])
    return pl.pallas_call(
        functools.partial(_post_kernel, n_mix=n_mix),
        grid=(t // tm, dff // tf),
        in_specs=in_specs,
        out_specs=pl.BlockSpec((tm, d), lambda i, k: (i, 0)),
        out_shape=jax.ShapeDtypeStruct((t, d), _F32),
        scratch_shapes=[pltpu.VMEM((tm, d), _F32), pltpu.VMEM((tm, d), _BF),
                        pltpu.VMEM((tm, d), _F32)],
        compiler_params=_params("parallel", "arbitrary"),
        name="post_mixer",
    )(x, *mix, *w_out_parts, g.reshape(1, d), w_up, w_down, p, w_p, p_g.reshape(1, d), w_g)


def _qk_prep_kernel(qkv_ref, cos_ref, sin_ref, qg_ref, kg_ref, q_ref, k_ref, vt_ref, km_ref):
    d = q_ref.shape[1]
    heads = d // _LANES
    j = pl.program_id(1)
    cos = cos_ref[...]
    sin = sin_ref[...]

    def rope(x, gain):
        xn = _rmsnorm(x, gain)
        return xn * cos + pltpu.roll(xn, _LANES // 2, axis=1) * sin

    for h in range(heads):
        sl = slice(h * _LANES, (h + 1) * _LANES)
        q_ref[:, sl] = rope(qkv_ref[:, sl], qg_ref[...])
        kr = rope(qkv_ref[:, d + h * _LANES:d + (h + 1) * _LANES], kg_ref[...])
        k_ref[:, sl] = kr.astype(_BF)
        km_ref[0, pl.ds(j, 1), :, sl] = jnp.mean(kr, axis=0, keepdims=True)[None]
    vt_ref[0, 0] = qkv_ref[:, 2 * d:].T.astype(_BF)


def _qk_prep(qkv, q_gain, k_gain, batch, seq):
    t = qkv.shape[0]
    d = qkv.shape[1] // 3
    blk = _MOBA_BLOCK
    nb = seq // blk
    half = _LANES // 2
    inv_freq = jnp.power(_ROPE_THETA, -jnp.arange(half, dtype=_F32) / half)
    ang = jnp.arange(seq, dtype=_F32)[:, None] * inv_freq[None, :]
    cos = jnp.concatenate([jnp.cos(ang), jnp.cos(ang)], axis=1)
    sin = jnp.concatenate([-jnp.sin(ang), jnp.sin(ang)], axis=1)
    tok = lambda w: pl.BlockSpec((blk, w), lambda b, j: (b * nb + j, 0))
    tab = pl.BlockSpec((blk, _LANES), lambda b, j: (j, 0))
    vec = pl.BlockSpec((1, _LANES), lambda b, j: (0, 0))
    return pl.pallas_call(
        _qk_prep_kernel,
        grid=(batch, nb),
        in_specs=[tok(3 * d), tab, tab, vec, vec],
        out_specs=[tok(d), tok(d), pl.BlockSpec((1, 1, d, blk), lambda b, j: (b, j, 0, 0)),
                   pl.BlockSpec((1, nb, 1, d), lambda b, j: (b, 0, 0, 0))],
        out_shape=[jax.ShapeDtypeStruct((t, d), _F32), jax.ShapeDtypeStruct((t, d), _BF),
                   jax.ShapeDtypeStruct((batch, nb, d, blk), _BF),
                   jax.ShapeDtypeStruct((batch, nb, 1, d), _F32)],
        compiler_params=_params("parallel", "arbitrary"),
        name="qk_prep",
    )(qkv, cos, sin, q_gain.reshape(1, _LANES), k_gain.reshape(1, _LANES))


def _moba_kernel(q_ref, k_ref, vt_ref, km_ref, o_ref, sel_s, m_s, l_s, acc_s):
    blk = _MOBA_BLOCK
    heads = q_ref.shape[1] // _LANES
    nb = km_ref.shape[1]
    i = pl.program_id(2)
    neg = -jnp.inf
    hs = range(heads)
    sls = [slice(h * _LANES, (h + 1) * _LANES) for h in hs]

    bidx = lax.broadcasted_iota(jnp.int32, (nb, blk), 0)
    qs = [q_ref[:, sls[h]] for h in hs]
    qts = [(qs[h] * (_LANES ** -0.5)).T.astype(_BF) for h in hs]
    gate = [_dot3(km_ref[0, :, sls[h]], qs[h], _NT) for h in hs]
    gate = [jnp.where(bidx < i, gate[h], neg) for h in hs]
    sel = [jnp.zeros((nb, blk), _F32) for h in hs]
    for _ in range(_MOBA_TOPK):
        m = [jnp.max(gate[h], axis=0, keepdims=True) for h in hs]
        cand = [jnp.where((gate[h] == m[h]) & (m[h] > neg), bidx, nb) for h in hs]
        pick = [bidx == jnp.min(cand[h], axis=0, keepdims=True) for h in hs]
        sel = [jnp.where(pick[h], 1.0, sel[h]) for h in hs]
        gate = [jnp.where(pick[h], neg, gate[h]) for h in hs]
    for h in hs:
        sel_s[h] = sel[h]

    ki = lax.broadcasted_iota(jnp.int32, (blk, blk), 0)
    qi = lax.broadcasted_iota(jnp.int32, (blk, blk), 1)
    causal = ki <= qi
    start = pl.multiple_of(i * blk, blk)
    s = [_mm(k_ref[pl.ds(start, blk), sls[h]], qts[h]) for h in hs]
    s = [jnp.where(causal, s[h], neg) for h in hs]
    m = [jnp.max(s[h], axis=0, keepdims=True) for h in hs]
    pr = [jnp.exp(s[h] - m[h]) for h in hs]
    pv = [_mm(vt_ref[0, i, sls[h], :], pr[h].astype(_BF)) for h in hs]
    for h in hs:
        m_s[h] = m[h]
        l_s[h] = jnp.sum(pr[h], axis=0, keepdims=True)
        acc_s[h] = pv[h]

    def past(j, carry):
        kstart = pl.multiple_of(j * blk, blk)
        s = [_mm(k_ref[pl.ds(kstart, blk), sls[h]], qts[h]) for h in hs]
        s = [jnp.where(sel_s[h, pl.ds(j, 1), :] > 0.5, s[h], neg) for h in hs]
        m_old = [m_s[h] for h in hs]
        m_new = [jnp.maximum(m_old[h], jnp.max(s[h], axis=0, keepdims=True)) for h in hs]
        alpha = [jnp.exp(m_old[h] - m_new[h]) for h in hs]
        pr = [jnp.exp(s[h] - m_new[h]) for h in hs]
        pv = [_mm(vt_ref[0, j, sls[h], :], pr[h].astype(_BF)) for h in hs]
        for h in hs:
            m_s[h] = m_new[h]
            l_s[h] = alpha[h] * l_s[h] + jnp.sum(pr[h], axis=0, keepdims=True)
            acc_s[h] = alpha[h] * acc_s[h] + pv[h]
        return carry

    lax.fori_loop(0, i, past, 0)
    for h in hs:
        o_ref[:, sls[h]] = (acc_s[h] / l_s[h]).T


def _moba(q, k, vt, kmean, batch, seq):
    t, d = q.shape
    blk = _MOBA_BLOCK
    nb = seq // blk
    w = _MOBA_HEADS_PER_STEP * _LANES
    qspec = pl.BlockSpec((blk, w), lambda b, h, i: (b * nb + i, h))
    return pl.pallas_call(
        _moba_kernel,
        grid=(batch, d // w, nb),
        in_specs=[qspec,
                  pl.BlockSpec((seq, w), lambda b, h, i: (b, h)),
                  pl.BlockSpec((1, nb, w, blk), lambda b, h, i: (b, 0, h, 0)),
                  pl.BlockSpec((1, nb, w), lambda b, h, i: (b, 0, h))],
        out_specs=qspec,
        out_shape=jax.ShapeDtypeStruct((t, d), _F32),
        scratch_shapes=[pltpu.VMEM((_MOBA_HEADS_PER_STEP, nb, blk), _F32),
                        pltpu.VMEM((_MOBA_HEADS_PER_STEP, 1, blk), _F32),
                        pltpu.VMEM((_MOBA_HEADS_PER_STEP, 1, blk), _F32),
                        pltpu.VMEM((_MOBA_HEADS_PER_STEP, _LANES, blk), _F32)],
        compiler_params=_params("parallel", "parallel", "arbitrary"),
        name="moba_attention",
    )(q, k, vt, kmean)


def kernel(x, p, attn_norm, mlp_norm, w_in_ar, w_out_ar, hgrn_lb, hgrn_onorm, rwkv_mu, rwkv_w0, rwkv_w2, rwkv_a0, rwkv_a2, rwkv_g2, rwkv_kk, rwkv_ka, rwkv_rk, rwkv_ln_w, rwkv_ln_b, w_qkv, w_o_attn, q_norm, k_norm, w_up, w_down, ple_proj, ple_norm, ple_gate):
    batch, seq, d = x.shape
    depth = p.shape[0]
    t = batch * seq
    hw = hgrn_onorm.shape[1]
    bf = lambda w: w.astype(_BF)
    lb_all = jnp.cumsum(jax.nn.softmax(hgrn_lb.astype(_F32), axis=0), axis=0)
    xt = x.reshape(t, d)
    for l in range(depth):
        if l % 2 == 0:
            e = l // 2
            u = _norm_matmul(xt, attn_norm[l], bf(w_in_ar[e]))
            o_a = _hgrn2(u, lb_all[l], hgrn_onorm[e], batch, seq)
            o_b = _rwkv7(u, 4 * hw, rwkv_mu[e], rwkv_w0[e], rwkv_w2[e], rwkv_a0[e], rwkv_a2[e],
                         rwkv_g2[e], rwkv_kk[e], rwkv_ka[e], rwkv_rk[e], rwkv_ln_w[e],
                         rwkv_ln_b[e], batch, seq)
            w_o = bf(w_out_ar[e])
            mix, w_parts = [o_a, o_b], [w_o[:hw], w_o[hw:]]
        else:
            o = l // 2
            qkv = _norm_matmul(xt, attn_norm[l], bf(w_qkv[o]))
            q, k, vt, kmean = _qk_prep(qkv, q_norm[o], k_norm[o], batch, seq)
            kmean = kmean.reshape(batch, -1, d)
            mix, w_parts = [_moba(q, k, vt, kmean, batch, seq)], [bf(w_o_attn[o])]
        xt = _post_mixer(xt, mix, w_parts, mlp_norm[l], bf(w_up[l]), bf(w_down[l]),
                         p[l].reshape(t, -1), bf(ple_proj[l]), ple_norm[l], bf(ple_gate[l]))
    return xt.reshape(batch, seq, d)
```

```python
import functools
import math

import jax
import jax.numpy as jnp
from jax import lax
from jax.experimental import pallas as pl
from jax.experimental.pallas import tpu as pltpu

_F32 = jnp.float32
_BF = jnp.bfloat16

_NORM_EPS = 1e-6
_RWKV_LN_EPS = 64e-5
_ROPE_THETA = 10000.0

_HGRN_HEAD = 128
_HGRN_CHUNK = 128
_HGRN_SUB = 16
_RWKV_HEAD = 64
_RWKV_CHUNK = 64
_RWKV_BLK = 16
_MOBA_BLOCK = 256
_MOBA_TOPK = 3
_MOBA_HEADS_PER_STEP = 8
_LANES = 128
_VMEM_LIMIT = 56 * 1024 * 1024

_NN = (((1,), (0,)), ((), ()))
_NT = (((1,), (1,)), ((), ()))
_TN = (((0,), (0,)), ((), ()))


def _mm(a, b, dims=_NN):
    return lax.dot_general(a, b, dims, preferred_element_type=_F32)


def _bdot(a, b, dims=_NN):
    return _mm(a.astype(_BF), b.astype(_BF), dims)


def _split2(a):
    hi = a.astype(_BF)
    lo = (a - hi.astype(_F32)).astype(_BF)
    return hi, lo


def _split3(a):
    hi = a.astype(_BF)
    r = a - hi.astype(_F32)
    mid = r.astype(_BF)
    lo = (r - mid.astype(_F32)).astype(_BF)
    return hi, mid, lo


def _dot3(a, b, dims=_NN):
    ah, al = _split2(a)
    bh, bl = _split2(b)
    return _mm(ah, bh, dims) + (_mm(ah, bl, dims) + _mm(al, bh, dims))


def _dot3_shared(lhs, b):
    bh, bl = _split2(b)
    parts = [_split2(a) for a in lhs]
    his = [h for h, _ in parts]
    tot = sum(a.shape[0] for a in lhs)
    r1 = _mm(jnp.concatenate(his + [l for _, l in parts], axis=0), bh)
    r2 = _mm(jnp.concatenate(his, axis=0), bl)
    out, off = [], 0
    for a in lhs:
        m = a.shape[0]
        out.append(r1[off:off + m] + (r1[tot + off:tot + off + m] + r2[off:off + m]))
        off += m
    return out


def _dot_exact_lhs(a_bf, b):
    w = b.shape[1]
    r = _mm(a_bf, jnp.concatenate(_split3(b), axis=1))
    return r[:, :w] + (r[:, w:2 * w] + r[:, 2 * w:])


def _dot_exact_rhs(a, b_bf):
    n = a.shape[0]
    r = _mm(jnp.concatenate(_split3(a), axis=0), b_bf)
    return r[:n] + (r[n:2 * n] + r[2 * n:])


def _rmsnorm(x, g):
    return x * lax.rsqrt(jnp.mean(x * x, axis=-1, keepdims=True) + _NORM_EPS) * g


def _sigmoid(x):
    return 1.0 / (1.0 + jnp.exp(-x))


def _silu(x):
    return x * _sigmoid(x)


def _params(*sem):
    return pltpu.CompilerParams(dimension_semantics=sem, vmem_limit_bytes=_VMEM_LIMIT)


def _norm_matmul_kernel(x_ref, g_ref, w_ref, o_ref):
    h = _rmsnorm(x_ref[...], g_ref[...]).astype(_BF)
    o_ref[...] = _mm(h, w_ref[...])


def _norm_matmul(x, g, w_bf, tm=256):
    t, d = x.shape
    n = w_bf.shape[1]
    return pl.pallas_call(
        _norm_matmul_kernel,
        grid=(t // tm,),
        in_specs=[
            pl.BlockSpec((tm, d), lambda i: (i, 0)),
            pl.BlockSpec((1, d), lambda i: (0, 0)),
            pl.BlockSpec((d, n), lambda i: (0, 0)),
        ],
        out_specs=pl.BlockSpec((tm, n), lambda i: (i, 0)),
        out_shape=jax.ShapeDtypeStruct((t, n), _F32),
        compiler_params=_params("parallel"),
        name="norm_matmul",
    )(x, g.reshape(1, d), w_bf)


def _hgrn_kernel(hq_ref, hf_ref, hi_ref, hg_ref, lb_ref, on_ref, o_ref,
                 st_ref, q_s, k_s, b_s, o_s):
    ch, sub = _HGRN_CHUNK, _HGRN_SUB
    width = hq_ref.shape[1]
    hs = range(width // _HGRN_HEAD)
    sls = [slice(h * _HGRN_HEAD, (h + 1) * _HGRN_HEAD) for h in hs]

    @pl.when(pl.program_id(1) == 0)
    def _():
        st_ref[...] = jnp.zeros_like(st_ref)

    lb = lb_ref[...]
    f = lb + (1.0 - lb) * _sigmoid(hf_ref[...])
    g = jnp.log(f)
    row = lax.broadcasted_iota(jnp.int32, (ch, ch), 0)
    col = lax.broadcasted_iota(jnp.int32, (ch, ch), 1)
    tri = jnp.where(row >= col, 1.0, 0.0).astype(_BF)
    b = _dot_exact_lhs(tri, g)
    q = _silu(hq_ref[...])
    k = 1.0 - f
    q_s[...] = q
    k_s[...] = k
    b_s[...] = b

    st = [st_ref[h] for h in hs]
    qe = q * jnp.exp(b)
    inter = [_bdot(qe[:, sls[h]], st[h], _NT) for h in hs]
    for h in hs:
        o_s[:, sls[h]] = inter[h]

    ones = jnp.ones((_HGRN_HEAD, _LANES), _BF)
    rsub = lax.broadcasted_iota(jnp.int32, (sub, width), 0)
    for i in range(ch // sub):
        r0 = i * sub
        rows = slice(r0, r0 + sub)
        b_i = b_s[rows, :]
        q_i = q_s[rows, :]
        o_i = [o_s[rows, sls[h]] for h in hs]
        if i > 0:
            base = b_s[r0 - 1:r0, :]
            qh = q_i * jnp.exp(b_i - base)
            kh = k_s[0:r0, :] * jnp.exp(base - b_s[0:r0, :])
            sc = [_bdot(qh[:, sls[h]], kh[:, sls[h]], _NT) for h in hs]
            off = [_bdot(sc[h], hi_ref[0:r0, sls[h]]) for h in hs]
            o_i = [o_i[h] + off[h] for h in hs]
        parts = []
        for s in range(sub):
            d = jnp.exp(jnp.minimum(b_i - b_s[r0 + s:r0 + s + 1, :], 0.0))
            p = q_i * d * k_s[r0 + s:r0 + s + 1, :]
            parts.append(jnp.where(rsub >= s, p, 0.0).astype(_BF))
        pst = jnp.concatenate(parts, axis=0)
        sb = [_mm(pst[:, sls[h]], ones) for h in hs]
        for s in range(sub):
            vrow = hi_ref[r0 + s:r0 + s + 1, :]
            o_i = [o_i[h] + sb[h][s * sub:(s + 1) * sub, :] * vrow[:, sls[h]] for h in hs]
        for h in hs:
            o_s[rows, sls[h]] = o_i[h]

    bl = b_s[ch - 1:ch, :]
    kd = k * jnp.exp(bl - b)
    ebl = jnp.exp(bl)
    upd = [_bdot(hi_ref[:, sls[h]], kd[:, sls[h]], _TN) for h in hs]
    for h in hs:
        st_ref[h] = st[h] * ebl[:, sls[h]] + upd[h]

    gate = _silu(hg_ref[...])
    for h in hs:
        o_ref[:, sls[h]] = _rmsnorm(o_s[:, sls[h]], on_ref[:, sls[h]]) * gate[:, sls[h]]


def _hgrn2(u, lb, onorm, batch, seq):
    t = u.shape[0]
    width = lb.shape[0]
    heads = width // _HGRN_HEAD
    ch = _HGRN_CHUNK
    nc = seq // ch

    def col(off):
        return pl.BlockSpec((ch, width), lambda b, c: (b * nc + c, off))

    vec = pl.BlockSpec((1, width), lambda b, c: (0, 0))
    tile = pltpu.VMEM((ch, width), _F32)
    return pl.pallas_call(
        _hgrn_kernel,
        grid=(batch, nc),
        in_specs=[col(0), col(1), col(2), col(3), vec, vec],
        out_specs=pl.BlockSpec((ch, width), lambda b, c: (b * nc + c, 0)),
        out_shape=jax.ShapeDtypeStruct((t, width), _F32),
        scratch_shapes=[pltpu.VMEM((heads, _HGRN_HEAD, _HGRN_HEAD), _F32), tile, tile, tile, tile],
        compiler_params=_params("parallel", "arbitrary"),
        name="hgrn2",
    )(u, u, u, u, lb.reshape(1, width), onorm.reshape(1, width))


def _tri_solve(a_list, rhs_list, n, blk):
    row = lax.broadcasted_iota(jnp.int32, (n, n), 0)
    col = lax.broadcasted_iota(jnp.int32, (n, n), 1)
    eye = jnp.where(row == col, 1.0, 0.0)
    shift = int(math.log2(blk))
    same = (row >> shift) == (col >> shift)
    ad = [jnp.where(same, a, 0.0) for a in a_list]
    low = [a - d for a, d in zip(a_list, ad)]
    ks = range(len(a_list))
    x = [eye + d for d in ad]
    p = [_dot3_shared([d], d)[0] for d in ad]
    for _ in range(shift - 2):
        both = [_dot3_shared([x[k], p[k]], p[k]) for k in ks]
        x = [x[k] + both[k][0] for k in ks]
        p = [both[k][1] for k in ks]
    x = [x[k] + _dot3_shared([x[k]], p[k])[0] for k in ks]
    both = [_dot3_shared([x[k]], jnp.concatenate([low[k], rhs_list[k]], axis=1))[0] for k in ks]
    nmat = [both[k][:, :n] for k in ks]
    term = [both[k][:, n:] for k in ks]
    u = term
    for _ in range((n // 2) // blk - 1):
        term = [_dot3_shared([nmat[k]], term[k])[0] for k in ks]
        u = [u[k] + term[k] for k in ks]
    return u


def _rwkv_kernel(r_ref, k_ref, v_ref, t_ref, mur_ref, muk_ref, muv_ref, mut_ref,
                 w0_ref, a0_ref, kk_ref, ka_ref, rk_ref, lnw_ref, lnb_ref,
                 w2_ref, a2_ref, g2_ref, o_ref,
                 st_ref, pr_ref, pk_ref, pv_ref, pt_ref):
    c = _RWKV_CHUNK
    pairs = r_ref.shape[1] // _LANES

    @pl.when(pl.program_id(1) == 0)
    def _():
        st_ref[...] = jnp.zeros_like(st_ref)
        pr_ref[...] = jnp.zeros_like(pr_ref)
        pk_ref[...] = jnp.zeros_like(pk_ref)
        pv_ref[...] = jnp.zeros_like(pv_ref)
        pt_ref[...] = jnp.zeros_like(pt_ref)

    def shift_mix(x_ref, prev_ref, mu_ref):
        x = x_ref[...]
        first = lax.broadcasted_iota(jnp.int32, x.shape, 0) == 0
        xs = jnp.where(first, prev_ref[...], pltpu.roll(x, 1, axis=0))
        prev_ref[...] = x_ref[c - 1:c, :]
        return x + (xs - x) * mu_ref[...]

    r = shift_mix(r_ref, pr_ref, mur_ref)
    k = shift_mix(k_ref, pk_ref, muk_ref)
    v = shift_mix(v_ref, pv_ref, muv_ref)
    tl = shift_mix(t_ref, pt_ref, mut_ref)
    lo = tl[:, :_LANES]
    gd = tl[:, _LANES:]

    wpre = w0_ref[...] + _dot3(jnp.tanh(lo), w2_ref[...])
    sp = jnp.maximum(-wpre, 0.0) + jnp.log(1.0 + jnp.exp(-jnp.abs(wpre)))
    logdec = -jnp.exp(-sp - 0.5)
    a = _sigmoid(a0_ref[...] + _dot3(lo, a2_ref[...]))
    g = _dot3(_sigmoid(gd), g2_ref[...])
    kkr = k * kk_ref[...]
    k2 = k * (1.0 + (a - 1.0) * ka_ref[...])
    rkk = r * k2 * rk_ref[...]

    n = 2 * c
    row = lax.broadcasted_iota(jnp.int32, (n, n), 0)
    col = lax.broadcasted_iota(jnp.int32, (n, n), 1)
    strict = row > col
    incl = row >= col
    trow = lax.broadcasted_iota(jnp.int32, (c, c), 0)
    tcol = lax.broadcasted_iota(jnp.int32, (c, c), 1)
    tri = jnp.where(trow >= tcol, 1.0, 0.0).astype(_BF)
    lane = lax.broadcasted_iota(jnp.int32, (1, _LANES), 1)
    head0 = lane < _RWKV_HEAD
    hrow = lax.broadcasted_iota(jnp.int32, (_LANES, _LANES), 0)
    hcol = lax.broadcasted_iota(jnp.int32, (_LANES, _LANES), 1)
    same_head = (hrow < _RWKV_HEAD) == (hcol < _RWKV_HEAD)
    head_ones = jnp.where(same_head, 1.0, 0.0).astype(_BF)

    def stack(x):
        return jnp.concatenate([jnp.where(head0, x, 0.0), jnp.where(head0, 0.0, x)], axis=0)

    def twice(x):
        return jnp.concatenate([x, x], axis=0)

    def unstack(x):
        return jnp.where(head0, x[:c], x[c:])

    ps = range(pairs)
    sls = [slice(p * _LANES, (p + 1) * _LANES) for p in ps]
    lw = [logdec[:, sl] for sl in sls]
    cum = [_dot_exact_lhs(tri, x) for x in lw]
    kk0 = [kkr[:, sl] for sl in sls]
    ss = [_dot_exact_rhs(x * x, head_ones) for x in kk0]
    kkp = [x / jnp.maximum(jnp.sqrt(q), 1e-12) for x, q in zip(kk0, ss)]
    ap = [a[:, sl] for sl in sls]
    k2p = [k2[:, sl] for sl in sls]
    rp = [r[:, sl] for sl in sls]
    vp = [v[:, sl] for sl in sls]
    vv = [twice(x) for x in vp]
    ginv = [jnp.exp(-x) for x in cum]
    at = [-kkp[p] * jnp.exp(cum[p] - lw[p]) for p in ps]
    bt = [kkp[p] * ap[p] * ginv[p] for p in ps]
    kt = [k2p[p] * ginv[p] for p in ps]
    rt = [rp[p] * jnp.exp(cum[p]) for p in ps]
    at2 = [stack(x) for x in at]
    rt2 = [stack(x) for x in rt]
    bt2 = [stack(x) for x in bt]
    kt2 = [stack(x) for x in kt]
    gq = [_bdot(jnp.concatenate([at2[p], rt2[p]], axis=0),
                jnp.concatenate([bt2[p], kt2[p]], axis=0), _NT) for p in ps]
    a_ab = [jnp.where(strict, gq[p][:n, :n], 0.0) for p in ps]
    a_ak = [jnp.where(strict, gq[p][:n, n:], 0.0) for p in ps]
    a_rb = [jnp.where(incl, gq[p][n:, :n], 0.0) for p in ps]
    a_rk = [jnp.where(incl, gq[p][n:, n:], 0.0) for p in ps]

    st = [st_ref[p] for p in ps]
    sx = [_bdot(jnp.concatenate([at[p], rt[p]], axis=0), st[p], _NT) for p in ps]
    av = [_bdot(jnp.concatenate([a_ak[p], a_rk[p]], axis=0), vv[p]) for p in ps]
    rhs = [twice(sx[p][:c]) + av[p][:n] for p in ps]
    u2 = _tri_solve(a_ab, rhs, n, _RWKV_BLK)
    uu = [unstack(x) for x in u2]
    y2 = [twice(sx[p][c:]) + _bdot(a_rb[p], twice(uu[p])) + av[p][n:] for p in ps]
    y = [unstack(x) for x in y2]

    cl = [x[c - 1:c, :] for x in cum]
    dl = [jnp.exp(cl[p] - cum[p]) for p in ps]
    upd = [_bdot(jnp.concatenate([uu[p], vp[p]], axis=0),
                 jnp.concatenate([kkp[p] * ap[p] * dl[p], k2p[p] * dl[p]], axis=0), _TN)
           for p in ps]
    for p in ps:
        st_ref[p] = jnp.where(same_head, st[p] * jnp.exp(cl[p]) + upd[p], 0.0)

    inv_n = 1.0 / _RWKV_HEAD
    mu = [_dot_exact_rhs(x, head_ones) * inv_n for x in y]
    yc = [y[p] - mu[p] for p in ps]
    var = [_dot_exact_rhs(x * x, head_ones) * inv_n for x in yc]
    bonus = [_dot_exact_rhs(rkk[:, sl], head_ones) for sl in sls]
    for p in ps:
        sl = sls[p]
        yn = yc[p] * lax.rsqrt(var[p] + _RWKV_LN_EPS) * lnw_ref[:, sl] + lnb_ref[:, sl]
        o_ref[:, sl] = (yn + bonus[p] * vp[p]) * g[:, sl]


def _rwkv7(u, col0, mu, w0, w2, a0, a2, g2, k_k, k_a, r_k, ln_w, ln_b, batch, seq):
    t = u.shape[0]
    width = w0.shape[0]
    c = _RWKV_CHUNK
    nc = seq // c
    tail = mu.shape[0] - 3 * width
    d_lora = w2.shape[0]
    a_lora = a2.shape[0]
    assert d_lora + a_lora == _LANES and tail - _LANES == g2.shape[0]
    assert col0 % width == 0 and (col0 + 3 * width) % tail == 0
    cb = col0 // width
    w2p = jnp.concatenate([w2, jnp.zeros((a_lora, width), _F32)], axis=0)
    a2p = jnp.concatenate([jnp.zeros((d_lora, width), _F32), a2], axis=0)

    def colblk(j):
        return pl.BlockSpec((c, width), lambda b, i: (b * nc + i, cb + j))

    def vec(nn):
        return pl.BlockSpec((1, nn), lambda b, i: (0, 0))

    def mat(m):
        return pl.BlockSpec(m.shape, lambda b, i: (0, 0))

    row = lambda x: x.reshape(1, -1)
    return pl.pallas_call(
        _rwkv_kernel,
        grid=(batch, nc),
        in_specs=[colblk(0), colblk(1), colblk(2),
                  pl.BlockSpec((c, tail), lambda b, i: (b * nc + i, (col0 + 3 * width) // tail)),
                  vec(width), vec(width), vec(width), vec(tail)]
        + [vec(width)] * 7 + [mat(w2p), mat(a2p), mat(g2)],
        out_specs=pl.BlockSpec((c, width), lambda b, i: (b * nc + i, 0)),
        out_shape=jax.ShapeDtypeStruct((t, width), _F32),
        scratch_shapes=[pltpu.VMEM((width // _LANES, _LANES, _LANES), _F32),
                        pltpu.VMEM((1, width), _F32), pltpu.VMEM((1, width), _F32),
                        pltpu.VMEM((1, width), _F32), pltpu.VMEM((1, tail), _F32)],
        compiler_params=_params("parallel", "arbitrary"),
        name="rwkv7",
    )(u, u, u, u,
      row(mu[:width]), row(mu[width:2 * width]), row(mu[2 * width:3 * width]), row(mu[3 * width:]),
      row(w0), row(a0), row(k_k), row(k_a), row(r_k), row(ln_w), row(ln_b), w2p, a2p, g2)


def _post_kernel(*refs, n_mix):
    x_ref = refs[0]
    o_refs = refs[1:1 + n_mix]
    wo_refs = refs[1 + n_mix:1 + 2 * n_mix]
    (g_ref, wup_ref, wdn_ref, p_ref, wp_ref, pg_ref, wg_ref, out_ref,
     x1_s, h_s, acc_s) = refs[1 + 2 * n_mix:]
    kf = pl.program_id(1)

    @pl.when(kf == 0)
    def _():
        x1 = x_ref[...]
        for o_ref, wo_ref in zip(o_refs, wo_refs):
            x1 = x1 + _mm(o_ref[...].astype(_BF), wo_ref[...])
        x1_s[...] = x1
        h_s[...] = _rmsnorm(x1, g_ref[...]).astype(_BF)
        acc_s[...] = jnp.zeros_like(acc_s)

    act = jnp.square(jnp.maximum(_mm(h_s[...], wup_ref[...]), 0.0))
    acc_s[...] += _mm(act.astype(_BF), wdn_ref[...])

    @pl.when(kf == pl.num_programs(1) - 1)
    def _():
        x2 = x1_s[...] + acc_s[...]
        ple = _rmsnorm(_mm(p_ref[...].astype(_BF), wp_ref[...]), pg_ref[...])
        gate = _sigmoid(_mm(x2.astype(_BF), wg_ref[...]))
        out_ref[...] = x2 + ple * gate


def _post_mixer(x, mix, w_out_parts, g, w_up, w_down, p, w_p, p_g, w_g, tm=512, tf=512):
    t, d = x.shape
    dff = w_up.shape[1]
    n_mix = len(mix)
    tok = lambda w: pl.BlockSpec((tm, w), lambda i, k: (i, 0))
    full = lambda m: pl.BlockSpec(m.shape, lambda i, k: (0, 0))
    in_specs = ([tok(d)] + [tok(o.shape[1]) for o in mix] + [full(w) for w in w_out_parts]
                + [pl.BlockSpec((1, d), lambda i, k: (0, 0)),
                   pl.BlockSpec((d, tf), lambda i, k: (0, k)),
                   pl.BlockSpec((tf, d), lambda i, k: (k, 0)),
                   tok(p.shape[1]), full(w_p),
                   pl.BlockSpec((1, d), lambda i, k: (0, 0)), full(w_g)])
    return pl.pallas_call(
        functools.partial(_post_kernel, n_mix=n_mix),
        grid=(t // tm, dff // tf),
        in_specs=in_specs,
        out_specs=pl.BlockSpec((tm, d), lambda i, k: (i, 0)),
        out_shape=jax.ShapeDtypeStruct((t, d), _F32),
        scratch_shapes=[pltpu.VMEM((tm, d), _F32), pltpu.VMEM((tm, d), _BF),
                        pltpu.VMEM((tm, d), _F32)],
        compiler_params=_params("parallel", "arbitrary"),
        name="post_mixer",
    )(x, *mix, *w_out_parts, g.reshape(1, d), w_up, w_down, p, w_p, p_g.reshape(1, d), w_g)


def _qk_prep_kernel(qkv_ref, cos_ref, sin_ref, qg_ref, kg_ref, q_ref, k_ref, vt_ref, km_ref):
    d = q_ref.shape[1]
    heads = d // _LANES
    j = pl.program_id(1)
    cos = cos_ref[...]
    sin = sin_ref[...]

    def rope(x, gain):
        xn = _rmsnorm(x, gain)
        return xn * cos + pltpu.roll(xn, _LANES // 2, axis=1) * sin

    for h in range(heads):
        sl = slice(h * _LANES, (h + 1) * _LANES)
        q_ref[:, sl] = rope(qkv_ref[:, sl], qg_ref[...])
        kr = rope(qkv_ref[:, d + h * _LANES:d + (h + 1) * _LANES], kg_ref[...])
        k_ref[:, sl] = kr.astype(_BF)
        km_ref[0, pl.ds(j, 1), :, sl] = jnp.mean(kr, axis=0, keepdims=True)[None]
    vt_ref[0, 0] = qkv_ref[:, 2 * d:].T.astype(_BF)


def _qk_prep(qkv, q_gain, k_gain, batch, seq):
    t = qkv.shape[0]
    d = qkv.shape[1] // 3
    blk = _MOBA_BLOCK
    nb = seq // blk
    half = _LANES // 2
    inv_freq = jnp.power(_ROPE_THETA, -jnp.arange(half, dtype=_F32) / half)
    ang = jnp.arange(seq, dtype=_F32)[:, None] * inv_freq[None, :]
    cos = jnp.concatenate([jnp.cos(ang), jnp.cos(ang)], axis=1)
    sin = jnp.concatenate([-jnp.sin(ang), jnp.sin(ang)], axis=1)
    tok = lambda w: pl.BlockSpec((blk, w), lambda b, j: (b * nb + j, 0))
    tab = pl.BlockSpec((blk, _LANES), lambda b, j: (j, 0))
    vec = pl.BlockSpec((1, _LANES), lambda b, j: (0, 0))
    return pl.pallas_call(
        _qk_prep_kernel,
        grid=(batch, nb),
        in_specs=[tok(3 * d), tab, tab, vec, vec],
        out_specs=[tok(d), tok(d), pl.BlockSpec((1, 1, d, blk), lambda b, j: (b, j, 0, 0)),
                   pl.BlockSpec((1, nb, 1, d), lambda b, j: (b, 0, 0, 0))],
        out_shape=[jax.ShapeDtypeStruct((t, d), _F32), jax.ShapeDtypeStruct((t, d), _BF),
                   jax.ShapeDtypeStruct((batch, nb, d, blk), _BF),
                   jax.ShapeDtypeStruct((batch, nb, 1, d), _F32)],
        compiler_params=_params("parallel", "arbitrary"),
        name="qk_prep",
    )(qkv, cos, sin, q_gain.reshape(1, _LANES), k_gain.reshape(1, _LANES))


def _moba_kernel(q_ref, k_ref, vt_ref, km_ref, o_ref, sel_s, m_s, l_s, acc_s):
    blk = _MOBA_BLOCK
    heads = q_ref.shape[1] // _LANES
    nb = km_ref.shape[1]
    i = pl.program_id(2)
    neg = -jnp.inf
    hs = range(heads)
    sls = [slice(h * _LANES, (h + 1) * _LANES) for h in hs]

    bidx = lax.broadcasted_iota(jnp.int32, (nb, blk), 0)
    qs = [q_ref[:, sls[h]] for h in hs]
    qts = [(qs[h] * (_LANES ** -0.5)).T.astype(_BF) for h in hs]
    gate = [_dot3(km_ref[0, :, sls[h]], qs[h], _NT) for h in hs]
    gate = [jnp.where(bidx < i, gate[h], neg) for h in hs]
    sel = [jnp.zeros((nb, blk), _F32) for h in hs]
    for _ in range(_MOBA_TOPK):
        m = [jnp.max(gate[h], axis=0, keepdims=True) for h in hs]
        cand = [jnp.where((gate[h] == m[h]) & (m[h] > neg), bidx, nb) for h in hs]
        pick = [bidx == jnp.min(cand[h], axis=0, keepdims=True) for h in hs]
        sel = [jnp.where(pick[h], 1.0, sel[h]) for h in hs]
        gate = [jnp.where(pick[h], neg, gate[h]) for h in hs]
    for h in hs:
        sel_s[h] = sel[h]

    ki = lax.broadcasted_iota(jnp.int32, (blk, blk), 0)
    qi = lax.broadcasted_iota(jnp.int32, (blk, blk), 1)
    causal = ki <= qi
    start = pl.multiple_of(i * blk, blk)
    s = [_mm(k_ref[pl.ds(start, blk), sls[h]], qts[h]) for h in hs]
    s = [jnp.where(causal, s[h], neg) for h in hs]
    m = [jnp.max(s[h], axis=0, keepdims=True) for h in hs]
    pr = [jnp.exp(s[h] - m[h]) for h in hs]
    pv = [_mm(vt_ref[0, i, sls[h], :], pr[h].astype(_BF)) for h in hs]
    for h in hs:
        m_s[h] = m[h]
        l_s[h] = jnp.sum(pr[h], axis=0, keepdims=True)
        acc_s[h] = pv[h]

    def past(j, carry):
        kstart = pl.multiple_of(j * blk, blk)
        s = [_mm(k_ref[pl.ds(kstart, blk), sls[h]], qts[h]) for h in hs]
        s = [jnp.where(sel_s[h, pl.ds(j, 1), :] > 0.5, s[h], neg) for h in hs]
        m_old = [m_s[h] for h in hs]
        m_new = [jnp.maximum(m_old[h], jnp.max(s[h], axis=0, keepdims=True)) for h in hs]
        alpha = [jnp.exp(m_old[h] - m_new[h]) for h in hs]
        pr = [jnp.exp(s[h] - m_new[h]) for h in hs]
        pv = [_mm(vt_ref[0, j, sls[h], :], pr[h].astype(_BF)) for h in hs]
        for h in hs:
            m_s[h] = m_new[h]
            l_s[h] = alpha[h] * l_s[h] + jnp.sum(pr[h], axis=0, keepdims=True)
            acc_s[h] = alpha[h] * acc_s[h] + pv[h]
        return carry

    lax.fori_loop(0, i, past, 0)
    for h in hs:
        o_ref[:, sls[h]] = (acc_s[h] / l_s[h]).T


def _moba(q, k, vt, kmean, batch, seq):
    t, d = q.shape
    blk = _MOBA_BLOCK
    nb = seq // blk
    w = _MOBA_HEADS_PER_STEP * _LANES
    qspec = pl.BlockSpec((blk, w), lambda b, h, i: (b * nb + i, h))
    return pl.pallas_call(
        _moba_kernel,
        grid=(batch, d // w, nb),
        in_specs=[qspec,
                  pl.BlockSpec((seq, w), lambda b, h, i: (b, h)),
                  pl.BlockSpec((1, nb, w, blk), lambda b, h, i: (b, 0, h, 0)),
                  pl.BlockSpec((1, nb, w), lambda b, h, i: (b, 0, h))],
        out_specs=qspec,
        out_shape=jax.ShapeDtypeStruct((t, d), _F32),
        scratch_shapes=[pltpu.VMEM((_MOBA_HEADS_PER_STEP, nb, blk), _F32),
                        pltpu.VMEM((_MOBA_HEADS_PER_STEP, 1, blk), _F32),
                        pltpu.VMEM((_MOBA_HEADS_PER_STEP, 1, blk), _F32),
                        pltpu.VMEM((_MOBA_HEADS_PER_STEP, _LANES, blk), _F32)],
        compiler_params=_params("parallel", "parallel", "arbitrary"),
        name="moba_attention",
    )(q, k, vt, kmean)


def kernel(x, p, attn_norm, mlp_norm, w_in_ar, w_out_ar, hgrn_lb, hgrn_onorm, rwkv_mu, rwkv_w0, rwkv_w2, rwkv_a0, rwkv_a2, rwkv_g2, rwkv_kk, rwkv_ka, rwkv_rk, rwkv_ln_w, rwkv_ln_b, w_qkv, w_o_attn, q_norm, k_norm, w_up, w_down, ple_proj, ple_norm, ple_gate):
    batch, seq, d = x.shape
    depth = p.shape[0]
    t = batch * seq
    hw = hgrn_onorm.shape[1]
    bf = lambda w: w.astype(_BF)
    lb_all = jnp.cumsum(jax.nn.softmax(hgrn_lb.astype(_F32), axis=0), axis=0)
    xt = x.reshape(t, d)
    for l in range(depth):
        if l % 2 == 0:
            e = l // 2
            u = _norm_matmul(xt, attn_norm[l], bf(w_in_ar[e]))
            o_a = _hgrn2(u, lb_all[l], hgrn_onorm[e], batch, seq)
            o_b = _rwkv7(u, 4 * hw, rwkv_mu[e], rwkv_w0[e], rwkv_w2[e], rwkv_a0[e], rwkv_a2[e],
                         rwkv_g2[e], rwkv_kk[e], rwkv_ka[e], rwkv_rk[e], rwkv_ln_w[e],
                         rwkv_ln_b[e], batch, seq)
            w_o = bf(w_out_ar[e])
            mix, w_parts = [o_a, o_b], [w_o[:hw], w_o[hw:]]
        else:
            o = l // 2
            qkv = _norm_matmul(xt, attn_norm[l], bf(w_qkv[o]))
            q, k, vt, kmean = _qk_prep(qkv, q_norm[o], k_norm[o], batch, seq)
            kmean = kmean.reshape(batch, -1, d)
            mix, w_parts = [_moba(q, k, vt, kmean, batch, seq)], [bf(w_o_attn[o])]
        xt = _post_mixer(xt, mix, w_parts, mlp_norm[l], bf(w_up[l]), bf(w_down[l]),
                         p[l].reshape(t, -1), bf(ple_proj[l]), ple_norm[l], bf(ple_gate[l]))
    return xt.reshape(batch, seq, d)
```

```python
import functools
import math

import jax
import jax.numpy as jnp
from jax import lax
from jax.experimental import pallas as pl
from jax.experimental.pallas import tpu as pltpu

_F32 = jnp.float32
_BF = jnp.bfloat16

_NORM_EPS = 1e-6
_RWKV_LN_EPS = 64e-5
_ROPE_THETA = 10000.0

_HGRN_HEAD = 128
_HGRN_CHUNK = 128
_HGRN_SUB = 16
_RWKV_HEAD = 64
_RWKV_CHUNK = 64
_RWKV_BLK = 16
_MOBA_BLOCK = 256
_MOBA_TOPK = 3
_MOBA_HEADS_PER_STEP = 8
_LANES = 128
_VMEM_LIMIT = 56 * 1024 * 1024

_NN = (((1,), (0,)), ((), ()))
_NT = (((1,), (1,)), ((), ()))
_TN = (((0,), (0,)), ((), ()))


def _mm(a, b, dims=_NN):
    return lax.dot_general(a, b, dims, preferred_element_type=_F32)


def _bdot(a, b, dims=_NN):
    return _mm(a.astype(_BF), b.astype(_BF), dims)


def _split2(a):
    hi = a.astype(_BF)
    lo = (a - hi.astype(_F32)).astype(_BF)
    return hi, lo


def _split3(a):
    hi = a.astype(_BF)
    r = a - hi.astype(_F32)
    mid = r.astype(_BF)
    lo = (r - mid.astype(_F32)).astype(_BF)
    return hi, mid, lo


def _dot3(a, b, dims=_NN):
    ah, al = _split2(a)
    bh, bl = _split2(b)
    return _mm(ah, bh, dims) + (_mm(ah, bl, dims) + _mm(al, bh, dims))


def _dot3_shared(lhs, b):
    bh, bl = _split2(b)
    parts = [_split2(a) for a in lhs]
    his = [h for h, _ in parts]
    tot = sum(a.shape[0] for a in lhs)
    r1 = _mm(jnp.concatenate(his + [l for _, l in parts], axis=0), bh)
    r2 = _mm(jnp.concatenate(his, axis=0), bl)
    out, off = [], 0
    for a in lhs:
        m = a.shape[0]
        out.append(r1[off:off + m] + (r1[tot + off:tot + off + m] + r2[off:off + m]))
        off += m
    return out


def _dot_exact_lhs(a_bf, b):
    w = b.shape[1]
    r = _mm(a_bf, jnp.concatenate(_split3(b), axis=1))
    return r[:, :w] + (r[:, w:2 * w] + r[:, 2 * w:])


def _dot_exact_rhs(a, b_bf):
    n = a.shape[0]
    r = _mm(jnp.concatenate(_split3(a), axis=0), b_bf)
    return r[:n] + (r[n:2 * n] + r[2 * n:])


def _rmsnorm(x, g):
    return x * lax.rsqrt(jnp.mean(x * x, axis=-1, keepdims=True) + _NORM_EPS) * g


def _sigmoid(x):
    return 1.0 / (1.0 + jnp.exp(-x))


def _silu(x):
    return x * _sigmoid(x)


def _params(*sem):
    return pltpu.CompilerParams(dimension_semantics=sem, vmem_limit_bytes=_VMEM_LIMIT)


def _norm_matmul_kernel(x_ref, g_ref, w_ref, o_ref):
    h = _rmsnorm(x_ref[...], g_ref[...]).astype(_BF)
    o_ref[...] = _mm(h, w_ref[...])


def _norm_matmul(x, g, w_bf, tm=256):
    t, d = x.shape
    n = w_bf.shape[1]
    return pl.pallas_call(
        _norm_matmul_kernel,
        grid=(t // tm,),
        in_specs=[
            pl.BlockSpec((tm, d), lambda i: (i, 0)),
            pl.BlockSpec((1, d), lambda i: (0, 0)),
            pl.BlockSpec((d, n), lambda i: (0, 0)),
        ],
        out_specs=pl.BlockSpec((tm, n), lambda i: (i, 0)),
        out_shape=jax.ShapeDtypeStruct((t, n), _F32),
        compiler_params=_params("parallel"),
        name="norm_matmul",
    )(x, g.reshape(1, d), w_bf)


def _hgrn_kernel(hq_ref, hf_ref, hi_ref, hg_ref, lb_ref, on_ref, o_ref,
                 st_ref, q_s, k_s, b_s, o_s):
    ch, sub = _HGRN_CHUNK, _HGRN_SUB
    width = hq_ref.shape[1]
    hs = range(width // _HGRN_HEAD)
    sls = [slice(h * _HGRN_HEAD, (h + 1) * _HGRN_HEAD) for h in hs]

    @pl.when(pl.program_id(1) == 0)
    def _():
        st_ref[...] = jnp.zeros_like(st_ref)

    lb = lb_ref[...]
    f = lb + (1.0 - lb) * _sigmoid(hf_ref[...])
    g = jnp.log(f)
    row = lax.broadcasted_iota(jnp.int32, (ch, ch), 0)
    col = lax.broadcasted_iota(jnp.int32, (ch, ch), 1)
    tri = jnp.where(row >= col, 1.0, 0.0).astype(_BF)
    b = _dot_exact_lhs(tri, g)
    q = _silu(hq_ref[...])
    k = 1.0 - f
    q_s[...] = q
    k_s[...] = k
    b_s[...] = b

    st = [st_ref[h] for h in hs]
    qe = q * jnp.exp(b)
    inter = [_bdot(qe[:, sls[h]], st[h], _NT) for h in hs]
    for h in hs:
        o_s[:, sls[h]] = inter[h]

    ones = jnp.ones((_HGRN_HEAD, _LANES), _BF)
    half = sub // 2
    rhalf = lax.broadcasted_iota(jnp.int32, (half, width), 0)
    for i in range(ch // sub):
        r0 = i * sub
        rows = slice(r0, r0 + sub)
        b_i = b_s[rows, :]
        q_i = q_s[rows, :]
        o_i = [o_s[rows, sls[h]] for h in hs]
        if i > 0:
            base = b_s[r0 - 1:r0, :]
            qh = q_i * jnp.exp(b_i - base)
            kh = k_s[0:r0, :] * jnp.exp(base - b_s[0:r0, :])
            sc = [_bdot(qh[:, sls[h]], kh[:, sls[h]], _NT) for h in hs]
            off = [_bdot(sc[h], hi_ref[0:r0, sls[h]]) for h in hs]
            o_i = [o_i[h] + off[h] for h in hs]
        b_lo, b_hi = b_i[:half], b_i[half:]
        q_lo, q_hi = q_i[:half], q_i[half:]
        parts = []
        for s in range(sub):
            brow = b_s[r0 + s:r0 + s + 1, :]
            krow = k_s[r0 + s:r0 + s + 1, :]
            if s < half:
                p_lo = q_lo * jnp.exp(jnp.minimum(b_lo - brow, 0.0)) * krow
                parts.append(jnp.where(rhalf >= s, p_lo, 0.0))
                parts.append(q_hi * jnp.exp(b_hi - brow) * krow)
            else:
                p_hi = q_hi * jnp.exp(jnp.minimum(b_hi - brow, 0.0)) * krow
                parts.append(jnp.where(rhalf >= s - half, p_hi, 0.0))
        pst = jnp.concatenate(parts, axis=0).astype(_BF)
        sb = [_mm(pst[:, sls[h]], ones) for h in hs]
        o_lo = [o_i[h][:half] for h in hs]
        o_hi = [o_i[h][half:] for h in hs]
        for s in range(sub):
            vrow = hi_ref[r0 + s:r0 + s + 1, :]
            if s < half:
                at = s * sub
                o_lo = [o_lo[h] + sb[h][at:at + half] * vrow[:, sls[h]] for h in hs]
                at += half
            else:
                at = half * sub + (s - half) * half
            o_hi = [o_hi[h] + sb[h][at:at + half] * vrow[:, sls[h]] for h in hs]
        for h in hs:
            o_s[r0:r0 + half, sls[h]] = o_lo[h]
            o_s[r0 + half:r0 + sub, sls[h]] = o_hi[h]

    bl = b_s[ch - 1:ch, :]
    kd = k * jnp.exp(bl - b)
    ebl = jnp.exp(bl)
    upd = [_bdot(hi_ref[:, sls[h]], kd[:, sls[h]], _TN) for h in hs]
    for h in hs:
        st_ref[h] = st[h] * ebl[:, sls[h]] + upd[h]

    gate = _silu(hg_ref[...])
    for h in hs:
        o_ref[:, sls[h]] = _rmsnorm(o_s[:, sls[h]], on_ref[:, sls[h]]) * gate[:, sls[h]]


def _hgrn2(u, lb, onorm, batch, seq):
    t = u.shape[0]
    width = lb.shape[0]
    heads = width // _HGRN_HEAD
    ch = _HGRN_CHUNK
    nc = seq // ch

    def col(off):
        return pl.BlockSpec((ch, width), lambda b, c: (b * nc + c, off))

    vec = pl.BlockSpec((1, width), lambda b, c: (0, 0))
    tile = pltpu.VMEM((ch, width), _F32)
    return pl.pallas_call(
        _hgrn_kernel,
        grid=(batch, nc),
        in_specs=[col(0), col(1), col(2), col(3), vec, vec],
        out_specs=pl.BlockSpec((ch, width), lambda b, c: (b * nc + c, 0)),
        out_shape=jax.ShapeDtypeStruct((t, width), _F32),
        scratch_shapes=[pltpu.VMEM((heads, _HGRN_HEAD, _HGRN_HEAD), _F32), tile, tile, tile, tile],
        compiler_params=_params("parallel", "arbitrary"),
        name="hgrn2",
    )(u, u, u, u, lb.reshape(1, width), onorm.reshape(1, width))


def _tri_solve(a_list, rhs_list, n, blk):
    row = lax.broadcasted_iota(jnp.int32, (n, n), 0)
    col = lax.broadcasted_iota(jnp.int32, (n, n), 1)
    eye = jnp.where(row == col, 1.0, 0.0)
    shift = int(math.log2(blk))
    same = (row >> shift) == (col >> shift)
    ad = [jnp.where(same, a, 0.0) for a in a_list]
    low = [a - d for a, d in zip(a_list, ad)]
    ks = range(len(a_list))
    x = [eye + d for d in ad]
    p = [_dot3_shared([d], d)[0] for d in ad]
    for _ in range(shift - 2):
        both = [_dot3_shared([x[k], p[k]], p[k]) for k in ks]
        x = [x[k] + both[k][0] for k in ks]
        p = [both[k][1] for k in ks]
    x = [x[k] + _dot3_shared([x[k]], p[k])[0] for k in ks]
    both = [_bdot(x[k], jnp.concatenate([low[k], rhs_list[k]], axis=1)) for k in ks]
    nmat = [both[k][:, :n].astype(_BF) for k in ks]
    term = [both[k][:, n:] for k in ks]
    u = term
    for _ in range((n // 2) // blk - 1):
        term = [_mm(nmat[k], term[k].astype(_BF)) for k in ks]
        u = [u[k] + term[k] for k in ks]
    return u


def _rwkv_kernel(r_ref, k_ref, v_ref, t_ref, mur_ref, muk_ref, muv_ref, mut_ref,
                 w0_ref, a0_ref, kk_ref, ka_ref, rk_ref, lnw_ref, lnb_ref,
                 w2_ref, a2_ref, g2_ref, o_ref,
                 st_ref, pr_ref, pk_ref, pv_ref, pt_ref):
    c = _RWKV_CHUNK
    pairs = r_ref.shape[1] // _LANES

    @pl.when(pl.program_id(1) == 0)
    def _():
        st_ref[...] = jnp.zeros_like(st_ref)
        pr_ref[...] = jnp.zeros_like(pr_ref)
        pk_ref[...] = jnp.zeros_like(pk_ref)
        pv_ref[...] = jnp.zeros_like(pv_ref)
        pt_ref[...] = jnp.zeros_like(pt_ref)

    def shift_mix(x_ref, prev_ref, mu_ref):
        x = x_ref[...]
        first = lax.broadcasted_iota(jnp.int32, x.shape, 0) == 0
        xs = jnp.where(first, prev_ref[...], pltpu.roll(x, 1, axis=0))
        prev_ref[...] = x_ref[c - 1:c, :]
        return x + (xs - x) * mu_ref[...]

    r = shift_mix(r_ref, pr_ref, mur_ref)
    k = shift_mix(k_ref, pk_ref, muk_ref)
    v = shift_mix(v_ref, pv_ref, muv_ref)
    tl = shift_mix(t_ref, pt_ref, mut_ref)
    lo = tl[:, :_LANES]
    gd = tl[:, _LANES:]

    wpre = w0_ref[...] + _dot3(jnp.tanh(lo), w2_ref[...])
    sp = jnp.maximum(-wpre, 0.0) + jnp.log(1.0 + jnp.exp(-jnp.abs(wpre)))
    logdec = -jnp.exp(-sp - 0.5)
    a = _sigmoid(a0_ref[...] + _dot3(lo, a2_ref[...]))
    g = _dot3(_sigmoid(gd), g2_ref[...])
    kkr = k * kk_ref[...]
    k2 = k * (1.0 + (a - 1.0) * ka_ref[...])
    rkk = r * k2 * rk_ref[...]

    n = 2 * c
    row = lax.broadcasted_iota(jnp.int32, (n, n), 0)
    col = lax.broadcasted_iota(jnp.int32, (n, n), 1)
    strict = row > col
    incl = row >= col
    trow = lax.broadcasted_iota(jnp.int32, (c, c), 0)
    tcol = lax.broadcasted_iota(jnp.int32, (c, c), 1)
    tri = jnp.where(trow >= tcol, 1.0, 0.0).astype(_BF)
    lane = lax.broadcasted_iota(jnp.int32, (1, _LANES), 1)
    head0 = lane < _RWKV_HEAD
    hrow = lax.broadcasted_iota(jnp.int32, (_LANES, _LANES), 0)
    hcol = lax.broadcasted_iota(jnp.int32, (_LANES, _LANES), 1)
    same_head = (hrow < _RWKV_HEAD) == (hcol < _RWKV_HEAD)
    head_ones = jnp.where(same_head, 1.0, 0.0).astype(_BF)

    def stack(x):
        return jnp.concatenate([jnp.where(head0, x, 0.0), jnp.where(head0, 0.0, x)], axis=0)

    def twice(x):
        return jnp.concatenate([x, x], axis=0)

    def unstack(x):
        return jnp.where(head0, x[:c], x[c:])

    ps = range(pairs)
    sls = [slice(p * _LANES, (p + 1) * _LANES) for p in ps]
    lw = [logdec[:, sl] for sl in sls]
    cum = [_dot_exact_lhs(tri, x) for x in lw]
    kk0 = [kkr[:, sl] for sl in sls]
    ss = [_dot_exact_rhs(x * x, head_ones) for x in kk0]
    kkp = [x / jnp.maximum(jnp.sqrt(q), 1e-12) for x, q in zip(kk0, ss)]
    ap = [a[:, sl] for sl in sls]
    k2p = [k2[:, sl] for sl in sls]
    rp = [r[:, sl] for sl in sls]
    vp = [v[:, sl] for sl in sls]
    vv = [twice(x) for x in vp]
    ginv = [jnp.exp(-x) for x in cum]
    at = [-kkp[p] * jnp.exp(cum[p] - lw[p]) for p in ps]
    bt = [kkp[p] * ap[p] * ginv[p] for p in ps]
    kt = [k2p[p] * ginv[p] for p in ps]
    rt = [rp[p] * jnp.exp(cum[p]) for p in ps]
    at2 = [stack(x) for x in at]
    rt2 = [stack(x) for x in rt]
    bt2 = [stack(x) for x in bt]
    kt2 = [stack(x) for x in kt]
    gq = [_bdot(jnp.concatenate([at2[p], rt2[p]], axis=0),
                jnp.concatenate([bt2[p], kt2[p]], axis=0), _NT) for p in ps]
    a_ab = [jnp.where(strict, gq[p][:n, :n], 0.0) for p in ps]
    a_ak = [jnp.where(strict, gq[p][:n, n:], 0.0) for p in ps]
    a_rb = [jnp.where(incl, gq[p][n:, :n], 0.0) for p in ps]
    a_rk = [jnp.where(incl, gq[p][n:, n:], 0.0) for p in ps]

    st = [st_ref[p] for p in ps]
    sx = [_bdot(jnp.concatenate([at[p], rt[p]], axis=0), st[p], _NT) for p in ps]
    av = [_bdot(jnp.concatenate([a_ak[p], a_rk[p]], axis=0), vv[p]) for p in ps]
    rhs = [twice(sx[p][:c]) + av[p][:n] for p in ps]
    u2 = _tri_solve(a_ab, rhs, n, _RWKV_BLK)
    uu = [unstack(x) for x in u2]
    y2 = [twice(sx[p][c:]) + _bdot(a_rb[p], twice(uu[p])) + av[p][n:] for p in ps]
    y = [unstack(x) for x in y2]

    cl = [x[c - 1:c, :] for x in cum]
    dl = [jnp.exp(cl[p] - cum[p]) for p in ps]
    upd = [_bdot(jnp.concatenate([uu[p], vp[p]], axis=0),
                 jnp.concatenate([kkp[p] * ap[p] * dl[p], k2p[p] * dl[p]], axis=0), _TN)
           for p in ps]
    for p in ps:
        st_ref[p] = jnp.where(same_head, st[p] * jnp.exp(cl[p]) + upd[p], 0.0)

    inv_n = 1.0 / _RWKV_HEAD
    mu = [_dot_exact_rhs(x, head_ones) * inv_n for x in y]
    yc = [y[p] - mu[p] for p in ps]
    var = [_dot_exact_rhs(x * x, head_ones) * inv_n for x in yc]
    bonus = [_dot_exact_rhs(rkk[:, sl], head_ones) for sl in sls]
    for p in ps:
        sl = sls[p]
        yn = yc[p] * lax.rsqrt(var[p] + _RWKV_LN_EPS) * lnw_ref[:, sl] + lnb_ref[:, sl]
        o_ref[:, sl] = (yn + bonus[p] * vp[p]) * g[:, sl]


def _rwkv7(u, col0, mu, w0, w2, a0, a2, g2, k_k, k_a, r_k, ln_w, ln_b, batch, seq):
    t = u.shape[0]
    width = w0.shape[0]
    c = _RWKV_CHUNK
    nc = seq // c
    tail = mu.shape[0] - 3 * width
    d_lora = w2.shape[0]
    a_lora = a2.shape[0]
    assert d_lora + a_lora == _LANES and tail - _LANES == g2.shape[0]
    assert col0 % width == 0 and (col0 + 3 * width) % tail == 0
    cb = col0 // width
    w2p = jnp.concatenate([w2, jnp.zeros((a_lora, width), _F32)], axis=0)
    a2p = jnp.concatenate([jnp.zeros((d_lora, width), _F32), a2], axis=0)

    def colblk(j):
        return pl.BlockSpec((c, width), lambda b, i: (b * nc + i, cb + j))

    def vec(nn):
        return pl.BlockSpec((1, nn), lambda b, i: (0, 0))

    def mat(m):
        return pl.BlockSpec(m.shape, lambda b, i: (0, 0))

    row = lambda x: x.reshape(1, -1)
    return pl.pallas_call(
        _rwkv_kernel,
        grid=(batch, nc),
        in_specs=[colblk(0), colblk(1), colblk(2),
                  pl.BlockSpec((c, tail), lambda b, i: (b * nc + i, (col0 + 3 * width) // tail)),
                  vec(width), vec(width), vec(width), vec(tail)]
        + [vec(width)] * 7 + [mat(w2p), mat(a2p), mat(g2)],
        out_specs=pl.BlockSpec((c, width), lambda b, i: (b * nc + i, 0)),
        out_shape=jax.ShapeDtypeStruct((t, width), _F32),
        scratch_shapes=[pltpu.VMEM((width // _LANES, _LANES, _LANES), _F32),
                        pltpu.VMEM((1, width), _F32), pltpu.VMEM((1, width), _F32),
                        pltpu.VMEM((1, width), _F32), pltpu.VMEM((1, tail), _F32)],
        compiler_params=_params("parallel", "arbitrary"),
        name="rwkv7",
    )(u, u, u, u,
      row(mu[:width]), row(mu[width:2 * width]), row(mu[2 * width:3 * width]), row(mu[3 * width:]),
      row(w0), row(a0), row(k_k), row(k_a), row(r_k), row(ln_w), row(ln_b), w2p, a2p, g2)


def _post_kernel(*refs, n_mix):
    x_ref = refs[0]
    o_refs = refs[1:1 + n_mix]
    wo_refs = refs[1 + n_mix:1 + 2 * n_mix]
    (g_ref, wup_ref, wdn_ref, p_ref, wp_ref, pg_ref, wg_ref, out_ref,
     x1_s, h_s, acc_s) = refs[1 + 2 * n_mix:]
    kf = pl.program_id(1)

    @pl.when(kf == 0)
    def _():
        x1 = x_ref[...]
        for o_ref, wo_ref in zip(o_refs, wo_refs):
            x1 = x1 + _mm(o_ref[...].astype(_BF), wo_ref[...])
        x1_s[...] = x1
        h_s[...] = _rmsnorm(x1, g_ref[...]).astype(_BF)
        acc_s[...] = jnp.zeros_like(acc_s)

    act = jnp.square(jnp.maximum(_mm(h_s[...], wup_ref[...]), 0.0))
    acc_s[...] += _mm(act.astype(_BF), wdn_ref[...])

    @pl.when(kf == pl.num_programs(1) - 1)
    def _():
        x2 = x1_s[...] + acc_s[...]
        ple = _rmsnorm(_mm(p_ref[...].astype(_BF), wp_ref[...]), pg_ref[...])
        gate = _sigmoid(_mm(x2.astype(_BF), wg_ref[...]))
        out_ref[...] = x2 + ple * gate


def _post_mixer(x, mix, w_out_parts, g, w_up, w_down, p, w_p, p_g, w_g, tm=512, tf=1024):
    t, d = x.shape
    dff = w_up.shape[1]
    n_mix = len(mix)
    tok = lambda w: pl.BlockSpec((tm, w), lambda i, k: (i, 0))
    full = lambda m: pl.BlockSpec(m.shape, lambda i, k: (0, 0))
    in_specs = ([tok(d)] + [tok(o.shape[1]) for o in mix] + [full(w) for w in w_out_parts]
                + [pl.BlockSpec((1, d), lambda i, k: (0, 0)),
                   pl.BlockSpec((d, tf), lambda i, k: (0, k)),
                   pl.BlockSpec((tf, d), lambda i, k: (k, 0)),
                   tok(p.shape[1]), full(w_p),
                   pl.BlockSpec((1, d), lambda i, k: (0, 0)), full(w_g)])
    return pl.pallas_call(
        functools.partial(_post_kernel, n_mix=n_mix),
        grid=(t // tm, dff // tf),
        in_specs=in_specs,
        out_specs=pl.BlockSpec((tm, d), lambda i, k: (i, 0)),
        out_shape=jax.ShapeDtypeStruct((t, d), _F32),
        scratch_shapes=[pltpu.VMEM((tm, d), _F32), pltpu.VMEM((tm, d), _BF),
                        pltpu.VMEM((tm, d), _F32)],
        compiler_params=_params("parallel", "arbitrary"),
        name="post_mixer",
    )(x, *mix, *w_out_parts, g.reshape(1, d), w_up, w_down, p, w_p, p_g.reshape(1, d), w_g)


def _qk_prep_kernel(qkv_ref, cos_ref, sin_ref, qg_ref, kg_ref, q_ref, k_ref, vt_ref, km_ref):
    d = q_ref.shape[1]
    heads = d // _LANES
    j = pl.program_id(1)
    cos = cos_ref[...]
    sin = sin_ref[...]

    def rope(x, gain):
        xn = _rmsnorm(x, gain)
        return xn * cos + pltpu.roll(xn, _LANES // 2, axis=1) * sin

    for h in range(heads):
        sl = slice(h * _LANES, (h + 1) * _LANES)
        q_ref[:, sl] = rope(qkv_ref[:, sl], qg_ref[...])
        kr = rope(qkv_ref[:, d + h * _LANES:d + (h + 1) * _LANES], kg_ref[...])
        k_ref[:, sl] = kr.astype(_BF)
        km_ref[0, pl.ds(j, 1), :, sl] = jnp.mean(kr, axis=0, keepdims=True)[None]
    vt_ref[0, 0] = qkv_ref[:, 2 * d:].T.astype(_BF)


def _qk_prep(qkv, q_gain, k_gain, batch, seq):
    t = qkv.shape[0]
    d = qkv.shape[1] // 3
    blk = _MOBA_BLOCK
    nb = seq // blk
    half = _LANES // 2
    inv_freq = jnp.power(_ROPE_THETA, -jnp.arange(half, dtype=_F32) / half)
    ang = jnp.arange(seq, dtype=_F32)[:, None] * inv_freq[None, :]
    cos = jnp.concatenate([jnp.cos(ang), jnp.cos(ang)], axis=1)
    sin = jnp.concatenate([-jnp.sin(ang), jnp.sin(ang)], axis=1)
    tok = lambda w: pl.BlockSpec((blk, w), lambda b, j: (b * nb + j, 0))
    tab = pl.BlockSpec((blk, _LANES), lambda b, j: (j, 0))
    vec = pl.BlockSpec((1, _LANES), lambda b, j: (0, 0))
    return pl.pallas_call(
        _qk_prep_kernel,
        grid=(batch, nb),
        in_specs=[tok(3 * d), tab, tab, vec, vec],
        out_specs=[tok(d), tok(d), pl.BlockSpec((1, 1, d, blk), lambda b, j: (b, j, 0, 0)),
                   pl.BlockSpec((1, nb, 1, d), lambda b, j: (b, 0, 0, 0))],
        out_shape=[jax.ShapeDtypeStruct((t, d), _F32), jax.ShapeDtypeStruct((t, d), _BF),
                   jax.ShapeDtypeStruct((batch, nb, d, blk), _BF),
                   jax.ShapeDtypeStruct((batch, nb, 1, d), _F32)],
        compiler_params=_params("parallel", "arbitrary"),
        name="qk_prep",
    )(qkv, cos, sin, q_gain.reshape(1, _LANES), k_gain.reshape(1, _LANES))


def _moba_kernel(q_ref, k_ref, vt_ref, km_ref, o_ref, sel_s, m_s, l_s, acc_s):
    blk = _MOBA_BLOCK
    heads = q_ref.shape[1] // _LANES
    nb = km_ref.shape[1]
    i = pl.program_id(2)
    neg = -jnp.inf
    hs = range(heads)
    sls = [slice(h * _LANES, (h + 1) * _LANES) for h in hs]

    bidx = lax.broadcasted_iota(jnp.int32, (nb, blk), 0)
    qs = [q_ref[:, sls[h]] for h in hs]
    qts = [(qs[h] * (_LANES ** -0.5)).T.astype(_BF) for h in hs]
    gate = [_dot3(km_ref[0, :, sls[h]], qs[h], _NT) for h in hs]
    gate = [jnp.where(bidx < i, gate[h], neg) for h in hs]
    sel = [jnp.zeros((nb, blk), _F32) for h in hs]
    for _ in range(_MOBA_TOPK):
        m = [jnp.max(gate[h], axis=0, keepdims=True) for h in hs]
        cand = [jnp.where((gate[h] == m[h]) & (m[h] > neg), bidx, nb) for h in hs]
        pick = [bidx == jnp.min(cand[h], axis=0, keepdims=True) for h in hs]
        sel = [jnp.where(pick[h], 1.0, sel[h]) for h in hs]
        gate = [jnp.where(pick[h], neg, gate[h]) for h in hs]
    for h in hs:
        sel_s[h] = sel[h]

    ki = lax.broadcasted_iota(jnp.int32, (blk, blk), 0)
    qi = lax.broadcasted_iota(jnp.int32, (blk, blk), 1)
    causal = ki <= qi
    start = pl.multiple_of(i * blk, blk)
    s = [_mm(k_ref[pl.ds(start, blk), sls[h]], qts[h]) for h in hs]
    s = [jnp.where(causal, s[h], neg) for h in hs]
    m = [jnp.max(s[h], axis=0, keepdims=True) for h in hs]
    ones = jnp.ones((16, blk), _BF)
    pr = [jnp.exp((s[h] - m[h]).astype(_BF)) for h in hs]
    pv = [_mm(jnp.concatenate([vt_ref[0, i, sls[h], :], ones], axis=0), pr[h]) for h in hs]
    for h in hs:
        m_s[h] = m[h]
        l_s[h] = pv[h][_LANES:_LANES + 1]
        acc_s[h] = pv[h][:_LANES]

    def past(j, carry):
        kstart = pl.multiple_of(j * blk, blk)
        s = [_mm(k_ref[pl.ds(kstart, blk), sls[h]], qts[h]) for h in hs]
        s = [jnp.where(sel_s[h, pl.ds(j, 1), :] > 0.5, s[h], neg) for h in hs]
        m_old = [m_s[h] for h in hs]
        m_new = [jnp.maximum(m_old[h], jnp.max(s[h], axis=0, keepdims=True)) for h in hs]
        alpha = [jnp.exp(m_old[h] - m_new[h]) for h in hs]
        pr = [jnp.exp((s[h] - m_new[h]).astype(_BF)) for h in hs]
        pv = [_mm(jnp.concatenate([vt_ref[0, j, sls[h], :], ones], axis=0), pr[h]) for h in hs]
        for h in hs:
            m_s[h] = m_new[h]
            l_s[h] = alpha[h] * l_s[h] + pv[h][_LANES:_LANES + 1]
            acc_s[h] = alpha[h] * acc_s[h] + pv[h][:_LANES]
        return carry

    lax.fori_loop(0, i, past, 0)
    for h in hs:
        o_ref[:, sls[h]] = (acc_s[h] / l_s[h]).T


def _moba(q, k, vt, kmean, batch, seq):
    t, d = q.shape
    blk = _MOBA_BLOCK
    nb = seq // blk
    w = _MOBA_HEADS_PER_STEP * _LANES
    qspec = pl.BlockSpec((blk, w), lambda b, h, i: (b * nb + i, h))
    return pl.pallas_call(
        _moba_kernel,
        grid=(batch, d // w, nb),
        in_specs=[qspec,
                  pl.BlockSpec((seq, w), lambda b, h, i: (b, h)),
                  pl.BlockSpec((1, nb, w, blk), lambda b, h, i: (b, 0, h, 0)),
                  pl.BlockSpec((1, nb, w), lambda b, h, i: (b, 0, h))],
        out_specs=qspec,
        out_shape=jax.ShapeDtypeStruct((t, d), _F32),
        scratch_shapes=[pltpu.VMEM((_MOBA_HEADS_PER_STEP, nb, blk), _F32),
                        pltpu.VMEM((_MOBA_HEADS_PER_STEP, 1, blk), _F32),
                        pltpu.VMEM((_MOBA_HEADS_PER_STEP, 1, blk), _F32),
                        pltpu.VMEM((_MOBA_HEADS_PER_STEP, _LANES, blk), _F32)],
        compiler_params=_params("parallel", "parallel", "arbitrary"),
        name="moba_attention",
    )(q, k, vt, kmean)


def kernel(x, p, attn_norm, mlp_norm, w_in_ar, w_out_ar, hgrn_lb, hgrn_onorm, rwkv_mu, rwkv_w0, rwkv_w2, rwkv_a0, rwkv_a2, rwkv_g2, rwkv_kk, rwkv_ka, rwkv_rk, rwkv_ln_w, rwkv_ln_b, w_qkv, w_o_attn, q_norm, k_norm, w_up, w_down, ple_proj, ple_norm, ple_gate):
    batch, seq, d = x.shape
    depth = p.shape[0]
    t = batch * seq
    hw = hgrn_onorm.shape[1]
    bf = lambda w: w.astype(_BF)
    lb_all = jnp.cumsum(jax.nn.softmax(hgrn_lb.astype(_F32), axis=0), axis=0)
    xt = x.reshape(t, d)
    for l in range(depth):
        if l % 2 == 0:
            e = l // 2
            u = _norm_matmul(xt, attn_norm[l], bf(w_in_ar[e]))
            o_a = _hgrn2(u, lb_all[l], hgrn_onorm[e], batch, seq)
            o_b = _rwkv7(u, 4 * hw, rwkv_mu[e], rwkv_w0[e], rwkv_w2[e], rwkv_a0[e], rwkv_a2[e],
                         rwkv_g2[e], rwkv_kk[e], rwkv_ka[e], rwkv_rk[e], rwkv_ln_w[e],
                         rwkv_ln_b[e], batch, seq)
            w_o = bf(w_out_ar[e])
            mix, w_parts = [o_a, o_b], [w_o[:hw], w_o[hw:]]
        else:
            o = l // 2
            qkv = _norm_matmul(xt, attn_norm[l], bf(w_qkv[o]))
            q, k, vt, kmean = _qk_prep(qkv, q_norm[o], k_norm[o], batch, seq)
            kmean = kmean.reshape(batch, -1, d)
            mix, w_parts = [_moba(q, k, vt, kmean, batch, seq)], [bf(w_o_attn[o])]
        xt = _post_mixer(xt, mix, w_parts, mlp_norm[l], bf(w_up[l]), bf(w_down[l]),
                         p[l].reshape(t, -1), bf(ple_proj[l]), ple_norm[l], bf(ple_gate[l]))
    return xt.reshape(batch, seq, d)
```

```python
import functools
import math

import jax
import jax.numpy as jnp
from jax import lax
from jax.experimental import pallas as pl
from jax.experimental.pallas import tpu as pltpu

_F32 = jnp.float32
_BF = jnp.bfloat16

_NORM_EPS = 1e-6
_RWKV_LN_EPS = 64e-5
_ROPE_THETA = 10000.0

_HGRN_HEAD = 128
_HGRN_CHUNK = 128
_HGRN_SUB = 16
_RWKV_HEAD = 64
_RWKV_CHUNK = 64
_RWKV_BLK = 16
_RWKV_BATCH_PER_STEP = 4
_MOBA_BLOCK = 256
_MOBA_TOPK = 3
_MOBA_HEADS_PER_STEP = 8
_LANES = 128
_VMEM_LIMIT = 56 * 1024 * 1024

_NN = (((1,), (0,)), ((), ()))
_NT = (((1,), (1,)), ((), ()))
_TN = (((0,), (0,)), ((), ()))


def _mm(a, b, dims=_NN):
    return lax.dot_general(a, b, dims, preferred_element_type=_F32)


def _bdot(a, b, dims=_NN):
    return _mm(a.astype(_BF), b.astype(_BF), dims)


def _split2(a):
    hi = a.astype(_BF)
    lo = (a - hi.astype(_F32)).astype(_BF)
    return hi, lo


def _split3(a):
    hi = a.astype(_BF)
    r = a - hi.astype(_F32)
    mid = r.astype(_BF)
    lo = (r - mid.astype(_F32)).astype(_BF)
    return hi, mid, lo


def _dot3(a, b, dims=_NN):
    ah, al = _split2(a)
    bh, bl = _split2(b)
    return _mm(ah, bh, dims) + (_mm(ah, bl, dims) + _mm(al, bh, dims))


def _dot3_shared(lhs, b):
    bh, bl = _split2(b)
    parts = [_split2(a) for a in lhs]
    his = [h for h, _ in parts]
    tot = sum(a.shape[0] for a in lhs)
    r1 = _mm(jnp.concatenate(his + [l for _, l in parts], axis=0), bh)
    r2 = _mm(jnp.concatenate(his, axis=0), bl)
    out, off = [], 0
    for a in lhs:
        m = a.shape[0]
        out.append(r1[off:off + m] + (r1[tot + off:tot + off + m] + r2[off:off + m]))
        off += m
    return out


def _dot_exact_lhs(a_bf, b):
    w = b.shape[1]
    r = _mm(a_bf, jnp.concatenate(_split3(b), axis=1))
    return r[:, :w] + (r[:, w:2 * w] + r[:, 2 * w:])


def _dot_exact_rhs(a, b_bf):
    n = a.shape[0]
    r = _mm(jnp.concatenate(_split3(a), axis=0), b_bf)
    return r[:n] + (r[n:2 * n] + r[2 * n:])


def _rmsnorm(x, g):
    return x * lax.rsqrt(jnp.mean(x * x, axis=-1, keepdims=True) + _NORM_EPS) * g


def _sigmoid(x):
    return 1.0 / (1.0 + jnp.exp(-x))


def _silu(x):
    return x * _sigmoid(x)


def _params(*sem):
    return pltpu.CompilerParams(dimension_semantics=sem, vmem_limit_bytes=_VMEM_LIMIT)


def _norm_matmul_kernel(x_ref, g_ref, w_ref, o_ref):
    h = _rmsnorm(x_ref[...], g_ref[...]).astype(_BF)
    o_ref[...] = _mm(h, w_ref[...])


def _norm_matmul(x, g, w_bf, tm=256):
    t, d = x.shape
    n = w_bf.shape[1]
    return pl.pallas_call(
        _norm_matmul_kernel,
        grid=(t // tm,),
        in_specs=[
            pl.BlockSpec((tm, d), lambda i: (i, 0)),
            pl.BlockSpec((1, d), lambda i: (0, 0)),
            pl.BlockSpec((d, n), lambda i: (0, 0)),
        ],
        out_specs=pl.BlockSpec((tm, n), lambda i: (i, 0)),
        out_shape=jax.ShapeDtypeStruct((t, n), _F32),
        compiler_params=_params("parallel"),
        name="norm_matmul",
    )(x, g.reshape(1, d), w_bf)


def _hgrn_kernel(hq_ref, hf_ref, hi_ref, hg_ref, lb_ref, on_ref, o_ref,
                 st_ref, q_s, k_s, b_s, o_s):
    ch, sub = _HGRN_CHUNK, _HGRN_SUB
    width = hq_ref.shape[1]
    hs = range(width // _HGRN_HEAD)
    sls = [slice(h * _HGRN_HEAD, (h + 1) * _HGRN_HEAD) for h in hs]

    @pl.when(pl.program_id(1) == 0)
    def _():
        st_ref[...] = jnp.zeros_like(st_ref)

    lb = lb_ref[...]
    f = lb + (1.0 - lb) * _sigmoid(hf_ref[...])
    g = jnp.log(f)
    row = lax.broadcasted_iota(jnp.int32, (ch, ch), 0)
    col = lax.broadcasted_iota(jnp.int32, (ch, ch), 1)
    tri = jnp.where(row >= col, 1.0, 0.0).astype(_BF)
    b = _dot_exact_lhs(tri, g)
    q = _silu(hq_ref[...])
    k = 1.0 - f
    q_s[...] = q
    k_s[...] = k
    b_s[...] = b

    st = [st_ref[h] for h in hs]
    qe = q * jnp.exp(b)
    inter = [_bdot(qe[:, sls[h]], st[h], _NT) for h in hs]
    for h in hs:
        o_s[:, sls[h]] = inter[h]

    ones = jnp.ones((_HGRN_HEAD, _LANES), _BF)
    half = sub // 2
    rhalf = lax.broadcasted_iota(jnp.int32, (half, width), 0)
    for i in range(ch // sub):
        r0 = i * sub
        rows = slice(r0, r0 + sub)
        b_i = b_s[rows, :]
        q_i = q_s[rows, :]
        o_i = [o_s[rows, sls[h]] for h in hs]
        if i > 0:
            base = b_s[r0 - 1:r0, :]
            qh = q_i * jnp.exp(b_i - base)
            kh = k_s[0:r0, :] * jnp.exp(base - b_s[0:r0, :])
            sc = [_bdot(qh[:, sls[h]], kh[:, sls[h]], _NT) for h in hs]
            off = [_bdot(sc[h], hi_ref[0:r0, sls[h]]) for h in hs]
            o_i = [o_i[h] + off[h] for h in hs]
        b_lo, b_hi = b_i[:half], b_i[half:]
        q_lo, q_hi = q_i[:half], q_i[half:]
        parts = []
        for s in range(sub):
            brow = b_s[r0 + s:r0 + s + 1, :]
            krow = k_s[r0 + s:r0 + s + 1, :]
            if s < half:
                p_lo = q_lo * jnp.exp(jnp.minimum(b_lo - brow, 0.0)) * krow
                parts.append(jnp.where(rhalf >= s, p_lo, 0.0))
                parts.append(q_hi * jnp.exp(b_hi - brow) * krow)
            else:
                p_hi = q_hi * jnp.exp(jnp.minimum(b_hi - brow, 0.0)) * krow
                parts.append(jnp.where(rhalf >= s - half, p_hi, 0.0))
        pst = jnp.concatenate(parts, axis=0).astype(_BF)
        sb = [_mm(pst[:, sls[h]], ones) for h in hs]
        o_lo = [o_i[h][:half] for h in hs]
        o_hi = [o_i[h][half:] for h in hs]
        for s in range(sub):
            vrow = hi_ref[r0 + s:r0 + s + 1, :]
            if s < half:
                at = s * sub
                o_lo = [o_lo[h] + sb[h][at:at + half] * vrow[:, sls[h]] for h in hs]
                at += half
            else:
                at = half * sub + (s - half) * half
            o_hi = [o_hi[h] + sb[h][at:at + half] * vrow[:, sls[h]] for h in hs]
        for h in hs:
            o_s[r0:r0 + half, sls[h]] = o_lo[h]
            o_s[r0 + half:r0 + sub, sls[h]] = o_hi[h]

    bl = b_s[ch - 1:ch, :]
    kd = k * jnp.exp(bl - b)
    ebl = jnp.exp(bl)
    upd = [_bdot(hi_ref[:, sls[h]], kd[:, sls[h]], _TN) for h in hs]
    for h in hs:
        st_ref[h] = st[h] * ebl[:, sls[h]] + upd[h]

    gate = _silu(hg_ref[...])
    for h in hs:
        o_ref[:, sls[h]] = _rmsnorm(o_s[:, sls[h]], on_ref[:, sls[h]]) * gate[:, sls[h]]


def _hgrn2(u, lb, onorm, batch, seq):
    t = u.shape[0]
    width = lb.shape[0]
    heads = width // _HGRN_HEAD
    ch = _HGRN_CHUNK
    nc = seq // ch

    def col(off):
        return pl.BlockSpec((ch, width), lambda b, c: (b * nc + c, off))

    vec = pl.BlockSpec((1, width), lambda b, c: (0, 0))
    tile = pltpu.VMEM((ch, width), _F32)
    return pl.pallas_call(
        _hgrn_kernel,
        grid=(batch, nc),
        in_specs=[col(0), col(1), col(2), col(3), vec, vec],
        out_specs=pl.BlockSpec((ch, width), lambda b, c: (b * nc + c, 0)),
        out_shape=jax.ShapeDtypeStruct((t, width), _F32),
        scratch_shapes=[pltpu.VMEM((heads, _HGRN_HEAD, _HGRN_HEAD), _F32), tile, tile, tile, tile],
        compiler_params=_params("parallel", "arbitrary"),
        name="hgrn2",
    )(u, u, u, u, lb.reshape(1, width), onorm.reshape(1, width))


def _tri_solve(a_list, rhs_list, n, blk):
    row = lax.broadcasted_iota(jnp.int32, (n, n), 0)
    col = lax.broadcasted_iota(jnp.int32, (n, n), 1)
    eye = jnp.where(row == col, 1.0, 0.0)
    shift = int(math.log2(blk))
    same = (row >> shift) == (col >> shift)
    ad = [jnp.where(same, a, 0.0) for a in a_list]
    low = [a - d for a, d in zip(a_list, ad)]
    ks = range(len(a_list))
    x = [eye + d for d in ad]
    p = [_dot3_shared([d], d)[0] for d in ad]
    for _ in range(shift - 2):
        both = [_dot3_shared([x[k], p[k]], p[k]) for k in ks]
        x = [x[k] + both[k][0] for k in ks]
        p = [both[k][1] for k in ks]
    x = [x[k] + _dot3_shared([x[k]], p[k])[0] for k in ks]
    both = [_bdot(x[k], jnp.concatenate([low[k], rhs_list[k]], axis=1)) for k in ks]
    nmat = [both[k][:, :n].astype(_BF) for k in ks]
    term = [both[k][:, n:] for k in ks]
    u = term
    for _ in range((n // 2) // blk - 1):
        term = [_mm(nmat[k], term[k].astype(_BF)) for k in ks]
        u = [u[k] + term[k] for k in ks]
    return u


def _rwkv_kernel(r_ref, k_ref, v_ref, t_ref, mur_ref, muk_ref, muv_ref, mut_ref,
                 w0_ref, a0_ref, kk_ref, ka_ref, rk_ref, lnw_ref, lnb_ref,
                 w2_ref, a2_ref, g2_ref, o_ref,
                 st_ref, pr_ref, pk_ref, pv_ref, pt_ref):
    c = _RWKV_CHUNK
    nbat = r_ref.shape[0]
    pairs = r_ref.shape[2] // _LANES

    @pl.when(pl.program_id(1) == 0)
    def _():
        st_ref[...] = jnp.zeros_like(st_ref)
        pr_ref[...] = jnp.zeros_like(pr_ref)
        pk_ref[...] = jnp.zeros_like(pk_ref)
        pv_ref[...] = jnp.zeros_like(pv_ref)
        pt_ref[...] = jnp.zeros_like(pt_ref)

    def shift_mix(x_ref, prev_ref, mu_ref):
        outs = []
        for bi in range(nbat):
            x = x_ref[bi]
            first = lax.broadcasted_iota(jnp.int32, x.shape, 0) == 0
            xs = jnp.where(first, prev_ref[bi], pltpu.roll(x, 1, axis=0))
            prev_ref[bi] = x_ref[bi, c - 1:c, :]
            outs.append(x + (xs - x) * mu_ref[...])
        return jnp.concatenate(outs, axis=0)

    r = shift_mix(r_ref, pr_ref, mur_ref)
    k = shift_mix(k_ref, pk_ref, muk_ref)
    v = shift_mix(v_ref, pv_ref, muv_ref)
    tl = shift_mix(t_ref, pt_ref, mut_ref)
    lo = tl[:, :_LANES]
    gd = tl[:, _LANES:]

    wpre = w0_ref[...] + _dot3(jnp.tanh(lo), w2_ref[...])
    sp = jnp.maximum(-wpre, 0.0) + jnp.log(1.0 + jnp.exp(-jnp.abs(wpre)))
    logdec = -jnp.exp(-sp - 0.5)
    a = _sigmoid(a0_ref[...] + _dot3(lo, a2_ref[...]))
    g = _dot3(_sigmoid(gd), g2_ref[...])
    kkr = k * kk_ref[...]
    k2 = k * (1.0 + (a - 1.0) * ka_ref[...])
    rkk = r * k2 * rk_ref[...]

    n = 2 * c
    row = lax.broadcasted_iota(jnp.int32, (n, n), 0)
    col = lax.broadcasted_iota(jnp.int32, (n, n), 1)
    strict = row > col
    incl = row >= col
    trow = lax.broadcasted_iota(jnp.int32, (c, c), 0)
    tcol = lax.broadcasted_iota(jnp.int32, (c, c), 1)
    tri = jnp.where(trow >= tcol, 1.0, 0.0).astype(_BF)
    lane = lax.broadcasted_iota(jnp.int32, (1, _LANES), 1)
    head0 = lane < _RWKV_HEAD
    hrow = lax.broadcasted_iota(jnp.int32, (_LANES, _LANES), 0)
    hcol = lax.broadcasted_iota(jnp.int32, (_LANES, _LANES), 1)
    same_head = (hrow < _RWKV_HEAD) == (hcol < _RWKV_HEAD)
    head_ones = jnp.where(same_head, 1.0, 0.0).astype(_BF)

    def stack(x):
        return jnp.concatenate([jnp.where(head0, x, 0.0), jnp.where(head0, 0.0, x)], axis=0)

    def twice(x):
        return jnp.concatenate([x, x], axis=0)

    def unstack(x):
        return jnp.where(head0, x[:c], x[c:])

    units = [(bi, p) for bi in range(nbat) for p in range(pairs)]
    ps = range(len(units))
    sls = [slice(p * _LANES, (p + 1) * _LANES) for _, p in units]

    def cut(x):
        return [x[bi * c:(bi + 1) * c, p * _LANES:(p + 1) * _LANES] for bi, p in units]

    lw = cut(logdec)
    cum = [_dot_exact_lhs(tri, x) for x in lw]
    kk0 = cut(kkr)
    ss = [_dot_exact_rhs(x * x, head_ones) for x in kk0]
    kkp = [x / jnp.maximum(jnp.sqrt(q), 1e-12) for x, q in zip(kk0, ss)]
    ap = cut(a)
    k2p = cut(k2)
    rp = cut(r)
    vp = cut(v)
    gp = cut(g)
    vv = [twice(x) for x in vp]
    ginv = [jnp.exp(-x) for x in cum]
    at = [-kkp[p] * jnp.exp(cum[p] - lw[p]) for p in ps]
    bt = [kkp[p] * ap[p] * ginv[p] for p in ps]
    kt = [k2p[p] * ginv[p] for p in ps]
    rt = [rp[p] * jnp.exp(cum[p]) for p in ps]
    at2 = [stack(x) for x in at]
    rt2 = [stack(x) for x in rt]
    bt2 = [stack(x) for x in bt]
    kt2 = [stack(x) for x in kt]
    gq = [_bdot(jnp.concatenate([at2[p], rt2[p]], axis=0),
                jnp.concatenate([bt2[p], kt2[p]], axis=0), _NT) for p in ps]
    a_ab = [jnp.where(strict, gq[p][:n, :n], 0.0) for p in ps]
    a_ak = [jnp.where(strict, gq[p][:n, n:], 0.0) for p in ps]
    a_rb = [jnp.where(incl, gq[p][n:, :n], 0.0) for p in ps]
    a_rk = [jnp.where(incl, gq[p][n:, n:], 0.0) for p in ps]

    st = [st_ref[p] for p in ps]
    sx = [_bdot(jnp.concatenate([at[p], rt[p]], axis=0), st[p], _NT) for p in ps]
    av = [_bdot(jnp.concatenate([a_ak[p], a_rk[p]], axis=0), vv[p]) for p in ps]
    rhs = [twice(sx[p][:c]) + av[p][:n] for p in ps]
    u2 = _tri_solve(a_ab, rhs, n, _RWKV_BLK)
    uu = [unstack(x) for x in u2]
    y2 = [twice(sx[p][c:]) + _bdot(a_rb[p], twice(uu[p])) + av[p][n:] for p in ps]
    y = [unstack(x) for x in y2]

    cl = [x[c - 1:c, :] for x in cum]
    dl = [jnp.exp(cl[p] - cum[p]) for p in ps]
    upd = [_bdot(jnp.concatenate([uu[p], vp[p]], axis=0),
                 jnp.concatenate([kkp[p] * ap[p] * dl[p], k2p[p] * dl[p]], axis=0), _TN)
           for p in ps]
    for p in ps:
        st_ref[p] = jnp.where(same_head, st[p] * jnp.exp(cl[p]) + upd[p], 0.0)

    inv_n = 1.0 / _RWKV_HEAD
    mu = [_dot_exact_rhs(x, head_ones) * inv_n for x in y]
    yc = [y[p] - mu[p] for p in ps]
    var = [_dot_exact_rhs(x * x, head_ones) * inv_n for x in yc]
    bonus = [_dot_exact_rhs(x, head_ones) for x in cut(rkk)]
    for p in ps:
        sl = sls[p]
        yn = yc[p] * lax.rsqrt(var[p] + _RWKV_LN_EPS) * lnw_ref[:, sl] + lnb_ref[:, sl]
        o_ref[units[p][0], :, sl] = (yn + bonus[p] * vp[p]) * gp[p]


def _rwkv7(u, col0, mu, w0, w2, a0, a2, g2, k_k, k_a, r_k, ln_w, ln_b, batch, seq):
    t = u.shape[0]
    width = w0.shape[0]
    c = _RWKV_CHUNK
    nc = seq // c
    tail = mu.shape[0] - 3 * width
    d_lora = w2.shape[0]
    a_lora = a2.shape[0]
    assert d_lora + a_lora == _LANES and tail - _LANES == g2.shape[0]
    assert col0 % width == 0 and (col0 + 3 * width) % tail == 0
    cb = col0 // width
    w2p = jnp.concatenate([w2, jnp.zeros((a_lora, width), _F32)], axis=0)
    a2p = jnp.concatenate([jnp.zeros((d_lora, width), _F32), a2], axis=0)

    nbat = math.gcd(batch, _RWKV_BATCH_PER_STEP)
    u3 =u.reshape(batch, seq, u.shape[1])

    def colblk(j):
        return pl.BlockSpec((nbat, c, width), lambda b, i: (b, i, cb + j))

    def vec(nn):
        return pl.BlockSpec((1, nn), lambda b, i: (0, 0))

    def mat(m):
        return pl.BlockSpec(m.shape, lambda b, i: (0, 0))

    row = lambda x: x.reshape(1, -1)
    return pl.pallas_call(
        _rwkv_kernel,
        grid=(batch // nbat, nc),
        in_specs=[colblk(0), colblk(1), colblk(2),
                  pl.BlockSpec((nbat, c, tail), lambda b, i: (b, i, (col0 + 3 * width) // tail)),
                  vec(width), vec(width), vec(width), vec(tail)]
        + [vec(width)] * 7 + [mat(w2p), mat(a2p), mat(g2)],
        out_specs=pl.BlockSpec((nbat, c, width), lambda b, i: (b, i, 0)),
        out_shape=jax.ShapeDtypeStruct((batch, seq, width), _F32),
        scratch_shapes=[pltpu.VMEM((nbat * width // _LANES, _LANES, _LANES), _F32),
                        pltpu.VMEM((nbat, 1, width), _F32), pltpu.VMEM((nbat, 1, width), _F32),
                        pltpu.VMEM((nbat, 1, width), _F32), pltpu.VMEM((nbat, 1, tail), _F32)],
        compiler_params=_params("parallel", "arbitrary"),
        name="rwkv7",
    )(u3, u3, u3, u3,
      row(mu[:width]), row(mu[width:2 * width]), row(mu[2 * width:3 * width]), row(mu[3 * width:]),
      row(w0), row(a0), row(k_k), row(k_a), row(r_k), row(ln_w), row(ln_b), w2p, a2p, g2
      ).reshape(t, width)


def _post_kernel(*refs, n_mix):
    x_ref = refs[0]
    o_refs = refs[1:1 + n_mix]
    wo_refs = refs[1 + n_mix:1 + 2 * n_mix]
    (g_ref, wup_ref, wdn_ref, p_ref, wp_ref, pg_ref, wg_ref, out_ref,
     x1_s, h_s, acc_s) = refs[1 + 2 * n_mix:]
    kf = pl.program_id(1)

    @pl.when(kf == 0)
    def _():
        x1 = x_ref[...]
        for o_ref, wo_ref in zip(o_refs, wo_refs):
            x1 = x1 + _mm(o_ref[...].astype(_BF), wo_ref[...])
        x1_s[...] = x1
        h_s[...] = _rmsnorm(x1, g_ref[...]).astype(_BF)
        acc_s[...] = jnp.zeros_like(acc_s)

    act = jnp.square(jnp.maximum(_mm(h_s[...], wup_ref[...]), 0.0))
    acc_s[...] += _mm(act.astype(_BF), wdn_ref[...])

    @pl.when(kf == pl.num_programs(1) - 1)
    def _():
        x2 = x1_s[...] + acc_s[...]
        ple = _rmsnorm(_mm(p_ref[...].astype(_BF), wp_ref[...]), pg_ref[...])
        gate = _sigmoid(_mm(x2.astype(_BF), wg_ref[...]))
        out_ref[...] = x2 + ple * gate


def _post_mixer(x, mix, w_out_parts, g, w_up, w_down, p, w_p, p_g, w_g, tm=512, tf=1024):
    t, d = x.shape
    dff = w_up.shape[1]
    n_mix = len(mix)
    tok = lambda w: pl.BlockSpec((tm, w), lambda i, k: (i, 0))
    full = lambda m: pl.BlockSpec(m.shape, lambda i, k: (0, 0))
    in_specs = ([tok(d)] + [tok(o.shape[1]) for o in mix] + [full(w) for w in w_out_parts]
                + [pl.BlockSpec((1, d), lambda i, k: (0, 0)),
                   pl.BlockSpec((d, tf), lambda i, k: (0, k)),
                   pl.BlockSpec((tf, d), lambda i, k: (k, 0)),
                   tok(p.shape[1]), full(w_p),
                   pl.BlockSpec((1, d), lambda i, k: (0, 0)), full(w_g)])
    return pl.pallas_call(
        functools.partial(_post_kernel, n_mix=n_mix),
        grid=(t // tm, dff // tf),
        in_specs=in_specs,
        out_specs=pl.BlockSpec((tm, d), lambda i, k: (i, 0)),
        out_shape=jax.ShapeDtypeStruct((t, d), _F32),
        scratch_shapes=[pltpu.VMEM((tm, d), _F32), pltpu.VMEM((tm, d), _BF),
                        pltpu.VMEM((tm, d), _F32)],
        compiler_params=_params("parallel", "arbitrary"),
        name="post_mixer",
    )(x, *mix, *w_out_parts, g.reshape(1, d), w_up, w_down, p, w_p, p_g.reshape(1, d), w_g)


def _qk_prep_kernel(x_ref, g_ref, w_ref, cos_ref, sin_ref, qg_ref, kg_ref,
                    q_ref, k_ref, vt_ref, km_ref):
    d = q_ref.shape[1]
    heads = d // _LANES
    j = pl.program_id(1)
    cos = cos_ref[...]
    sin = sin_ref[...]
    qkv = _mm(_rmsnorm(x_ref[...], g_ref[...]).astype(_BF), w_ref[...])

    def rope(x, gain):
        xn = _rmsnorm(x, gain)
        return xn * cos + pltpu.roll(xn, _LANES // 2, axis=1) * sin

    for h in range(heads):
        sl = slice(h * _LANES, (h + 1) * _LANES)
        q_ref[:, sl] = rope(qkv[:, sl], qg_ref[...])
        kr = rope(qkv[:, d + h * _LANES:d + (h + 1) * _LANES], kg_ref[...])
        k_ref[:, sl] = kr.astype(_BF)
        km_ref[0, pl.ds(j, 1), :, sl] = jnp.mean(kr, axis=0, keepdims=True)[None]
    vt_ref[0, 0] = qkv[:, 2 * d:].T.astype(_BF)


def _qk_prep(x, g, w_bf, q_gain, k_gain, batch, seq):
    t, d = x.shape
    blk = _MOBA_BLOCK
    nb = seq // blk
    half = _LANES // 2
    inv_freq = jnp.power(_ROPE_THETA, -jnp.arange(half, dtype=_F32) / half)
    ang = jnp.arange(seq, dtype=_F32)[:, None] * inv_freq[None, :]
    cos = jnp.concatenate([jnp.cos(ang), jnp.cos(ang)], axis=1)
    sin = jnp.concatenate([-jnp.sin(ang), jnp.sin(ang)], axis=1)
    tok = lambda w: pl.BlockSpec((blk, w), lambda b, j: (b * nb + j, 0))
    tab = pl.BlockSpec((blk, _LANES), lambda b, j: (j, 0))
    vec = pl.BlockSpec((1, _LANES), lambda b, j: (0, 0))
    return pl.pallas_call(
        _qk_prep_kernel,
        grid=(batch, nb),
        in_specs=[tok(d), pl.BlockSpec((1, d), lambda b, j: (0, 0)),
                  pl.BlockSpec(w_bf.shape, lambda b, j: (0, 0)), tab, tab, vec, vec],
        out_specs=[tok(d), tok(d), pl.BlockSpec((1, 1, d, blk), lambda b, j: (b, j, 0, 0)),
                   pl.BlockSpec((1, nb, 1, d), lambda b, j: (b, 0, 0, 0))],
        out_shape=[jax.ShapeDtypeStruct((t, d), _F32), jax.ShapeDtypeStruct((t, d), _BF),
                   jax.ShapeDtypeStruct((batch, nb, d, blk), _BF),
                   jax.ShapeDtypeStruct((batch, nb, 1, d), _F32)],
        compiler_params=_params("parallel", "arbitrary"),
        name="qk_prep",
    )(x, g.reshape(1, d), w_bf, cos, sin, q_gain.reshape(1, _LANES), k_gain.reshape(1, _LANES))


def _moba_kernel(q_ref, k_ref, vt_ref, km_ref, o_ref, sel_s, m_s, l_s, acc_s):
    blk = _MOBA_BLOCK
    heads = q_ref.shape[1] // _LANES
    nb = km_ref.shape[1]
    i = pl.program_id(2)
    neg = -jnp.inf
    hs = range(heads)
    sls = [slice(h * _LANES, (h + 1) * _LANES) for h in hs]

    bidx = lax.broadcasted_iota(jnp.int32, (nb, blk), 0)
    qs = [q_ref[:, sls[h]] for h in hs]
    qts = [(qs[h] * (_LANES ** -0.5)).T.astype(_BF) for h in hs]
    gate = [_dot3(km_ref[0, :, sls[h]], qs[h], _NT) for h in hs]
    gate = [jnp.where(bidx < i, gate[h], neg) for h in hs]
    sel = [jnp.zeros((nb, blk), _F32) for h in hs]
    for _ in range(_MOBA_TOPK):
        m = [jnp.max(gate[h], axis=0, keepdims=True) for h in hs]
        cand = [jnp.where((gate[h] == m[h]) & (m[h] > neg), bidx, nb) for h in hs]
        pick = [bidx == jnp.min(cand[h], axis=0, keepdims=True) for h in hs]
        sel = [jnp.where(pick[h], 1.0, sel[h]) for h in hs]
        gate = [jnp.where(pick[h], neg, gate[h]) for h in hs]
    for h in hs:
        sel_s[h] = sel[h]

    ki = lax.broadcasted_iota(jnp.int32, (blk, blk), 0)
    qi = lax.broadcasted_iota(jnp.int32, (blk, blk), 1)
    causal = ki <= qi
    start = pl.multiple_of(i * blk, blk)
    s = [_mm(k_ref[pl.ds(start, blk), sls[h]], qts[h]) for h in hs]
    s = [jnp.where(causal, s[h], neg) for h in hs]
    m = [jnp.max(s[h], axis=0, keepdims=True) for h in hs]
    ones = jnp.ones((16, blk), _BF)
    pr = [jnp.exp((s[h] - m[h]).astype(_BF)) for h in hs]
    pv = [_mm(jnp.concatenate([vt_ref[0, i, sls[h], :], ones], axis=0), pr[h]) for h in hs]
    for h in hs:
        m_s[h] = m[h]
        l_s[h] = pv[h][_LANES:_LANES + 1]
        acc_s[h] = pv[h][:_LANES]

    def past(j, carry):
        kstart = pl.multiple_of(j * blk, blk)
        s = [_mm(k_ref[pl.ds(kstart, blk), sls[h]], qts[h]) for h in hs]
        s = [jnp.where(sel_s[h, pl.ds(j, 1), :] > 0.5, s[h], neg) for h in hs]
        m_old = [m_s[h] for h in hs]
        m_new = [jnp.maximum(m_old[h], jnp.max(s[h], axis=0, keepdims=True)) for h in hs]
        alpha = [jnp.exp(m_old[h] - m_new[h]) for h in hs]
        pr = [jnp.exp((s[h] - m_new[h]).astype(_BF)) for h in hs]
        pv = [_mm(jnp.concatenate([vt_ref[0, j, sls[h], :], ones], axis=0), pr[h]) for h in hs]
        for h in hs:
            m_s[h] = m_new[h]
            l_s[h] = alpha[h] * l_s[h] + pv[h][_LANES:_LANES + 1]
            acc_s[h] = alpha[h] * acc_s[h] + pv[h][:_LANES]
        return carry

    lax.fori_loop(0, i, past, 0)
    for h in hs:
        o_ref[:, sls[h]] = (acc_s[h] / l_s[h]).T


def _moba(q, k, vt, kmean, batch, seq):
    t, d = q.shape
    blk = _MOBA_BLOCK
    nb = seq // blk
    w = _MOBA_HEADS_PER_STEP * _LANES
    qspec = pl.BlockSpec((blk, w), lambda b, h, i: (b * nb + i, h))
    return pl.pallas_call(
        _moba_kernel,
        grid=(batch, d // w, nb),
        in_specs=[qspec,
                  pl.BlockSpec((seq, w), lambda b, h, i: (b, h)),
                  pl.BlockSpec((1, nb, w, blk), lambda b, h, i: (b, 0, h, 0)),
                  pl.BlockSpec((1, nb, w), lambda b, h, i: (b, 0, h))],
        out_specs=qspec,
        out_shape=jax.ShapeDtypeStruct((t, d), _F32),
        scratch_shapes=[pltpu.VMEM((_MOBA_HEADS_PER_STEP, nb, blk), _F32),
                        pltpu.VMEM((_MOBA_HEADS_PER_STEP, 1, blk), _F32),
                        pltpu.VMEM((_MOBA_HEADS_PER_STEP, 1, blk), _F32),
                        pltpu.VMEM((_MOBA_HEADS_PER_STEP, _LANES, blk), _F32)],
        compiler_params=_params("parallel", "parallel", "arbitrary"),
        name="moba_attention",
    )(q, k, vt, kmean)


def kernel(x, p, attn_norm, mlp_norm, w_in_ar, w_out_ar, hgrn_lb, hgrn_onorm, rwkv_mu, rwkv_w0, rwkv_w2, rwkv_a0, rwkv_a2, rwkv_g2, rwkv_kk, rwkv_ka, rwkv_rk, rwkv_ln_w, rwkv_ln_b, w_qkv, w_o_attn, q_norm, k_norm, w_up, w_down, ple_proj, ple_norm, ple_gate):
    batch, seq, d = x.shape
    depth = p.shape[0]
    t = batch * seq
    hw = hgrn_onorm.shape[1]
    bf = lambda w: w.astype(_BF)
    lb_all = jnp.cumsum(jax.nn.softmax(hgrn_lb.astype(_F32), axis=0), axis=0)
    xt = x.reshape(t, d)
    for l in range(depth):
        if l % 2 == 0:
            e = l // 2
            u = _norm_matmul(xt, attn_norm[l], bf(w_in_ar[e]))
            o_a = _hgrn2(u, lb_all[l], hgrn_onorm[e], batch, seq)
            o_b = _rwkv7(u, 4 * hw, rwkv_mu[e], rwkv_w0[e], rwkv_w2[e], rwkv_a0[e], rwkv_a2[e],
                         rwkv_g2[e], rwkv_kk[e], rwkv_ka[e], rwkv_rk[e], rwkv_ln_w[e],
                         rwkv_ln_b[e], batch, seq)
            w_o = bf(w_out_ar[e])
            mix, w_parts = [o_a, o_b], [w_o[:hw], w_o[hw:]]
        else:
            o = l // 2
            q, k, vt, kmean = _qk_prep(xt, attn_norm[l], bf(w_qkv[o]), q_norm[o], k_norm[o],
                                       batch, seq)
            kmean = kmean.reshape(batch, -1, d)
            mix, w_parts = [_moba(q, k, vt, kmean, batch, seq)], [bf(w_o_attn[o])]
        xt = _post_mixer(xt, mix, w_parts, mlp_norm[l], bf(w_up[l]), bf(w_down[l]),
                         p[l].reshape(t, -1), bf(ple_proj[l]), ple_norm[l], bf(ple_gate[l]))
    return xt.reshape(batch, seq, d)
```

```python
import functools
import math

import jax
import jax.numpy as jnp
from jax import lax
from jax.experimental import pallas as pl
from jax.experimental.pallas import tpu as pltpu

_F32 = jnp.float32
_BF = jnp.bfloat16

_NORM_EPS = 1e-6
_RWKV_LN_EPS = 64e-5
_ROPE_THETA = 10000.0

_HGRN_HEAD = 128
_HGRN_CHUNK = 128
_HGRN_SUB = 16
_RWKV_HEAD = 64
_RWKV_CHUNK = 64
_RWKV_BLK = 16
_RWKV_BATCH_PER_STEP = 4
_MOBA_BLOCK = 256
_MOBA_TOPK = 3
_MOBA_HEADS_PER_STEP = 8
_LANES = 128
_VMEM_LIMIT = 56 * 1024 * 1024

_NN = (((1,), (0,)), ((), ()))
_NT = (((1,), (1,)), ((), ()))
_TN = (((0,), (0,)), ((), ()))


def _mm(a, b, dims=_NN):
    return lax.dot_general(a, b, dims, preferred_element_type=_F32)


def _bdot(a, b, dims=_NN):
    return _mm(a.astype(_BF), b.astype(_BF), dims)


def _split2(a):
    hi = a.astype(_BF)
    lo = (a - hi.astype(_F32)).astype(_BF)
    return hi, lo


def _split3(a):
    hi = a.astype(_BF)
    r = a - hi.astype(_F32)
    mid = r.astype(_BF)
    lo = (r - mid.astype(_F32)).astype(_BF)
    return hi, mid, lo


def _dot3(a, b, dims=_NN):
    ah, al = _split2(a)
    bh, bl = _split2(b)
    return _mm(ah, bh, dims) + (_mm(ah, bl, dims) + _mm(al, bh, dims))


def _dot3_shared(lhs, b):
    bh, bl = _split2(b)
    parts = [_split2(a) for a in lhs]
    his = [h for h, _ in parts]
    tot = sum(a.shape[0] for a in lhs)
    r1 = _mm(jnp.concatenate(his + [l for _, l in parts], axis=0), bh)
    r2 = _mm(jnp.concatenate(his, axis=0), bl)
    out, off = [], 0
    for a in lhs:
        m = a.shape[0]
        out.append(r1[off:off + m] + (r1[tot + off:tot + off + m] + r2[off:off + m]))
        off += m
    return out


def _dot_exact_lhs(a_bf, b):
    w = b.shape[1]
    r = _mm(a_bf, jnp.concatenate(_split3(b), axis=1))
    return r[:, :w] + (r[:, w:2 * w] + r[:, 2 * w:])


def _dot_exact_rhs(a, b_bf):
    n = a.shape[0]
    r = _mm(jnp.concatenate(_split3(a), axis=0), b_bf)
    return r[:n] + (r[n:2 * n] + r[2 * n:])


def _rmsnorm(x, g):
    return x * lax.rsqrt(jnp.mean(x * x, axis=-1, keepdims=True) + _NORM_EPS) * g


def _sigmoid(x):
    return 1.0 / (1.0 + jnp.exp(-x))


def _silu(x):
    return x * _sigmoid(x)


def _params(*sem):
    return pltpu.CompilerParams(dimension_semantics=sem, vmem_limit_bytes=_VMEM_LIMIT)


def _norm_matmul_kernel(x_ref, g_ref, w_ref, o_ref):
    h = _rmsnorm(x_ref[...], g_ref[...]).astype(_BF)
    o_ref[...] = _mm(h, w_ref[...])


def _norm_matmul(x, g, w_bf, tm=512):
    t, d = x.shape
    n = w_bf.shape[1]
    return pl.pallas_call(
        _norm_matmul_kernel,
        grid=(t // tm,),
        in_specs=[
            pl.BlockSpec((tm, d), lambda i: (i, 0)),
            pl.BlockSpec((1, d), lambda i: (0, 0)),
            pl.BlockSpec((d, n), lambda i: (0, 0)),
        ],
        out_specs=pl.BlockSpec((tm, n), lambda i: (i, 0)),
        out_shape=jax.ShapeDtypeStruct((t, n), _F32),
        compiler_params=_params("parallel"),
        name="norm_matmul",
    )(x, g.reshape(1, d), w_bf)


def _hgrn_kernel(hq_ref, hf_ref, hi_ref, hg_ref, lb_ref, on_ref, o_ref,
                 st_ref, q_s, k_s, b_s, c_s, o_s):
    ch, sub = _HGRN_CHUNK, _HGRN_SUB
    width = hq_ref.shape[1]
    hs = range(width // _HGRN_HEAD)
    sls = [slice(h * _HGRN_HEAD, (h + 1) * _HGRN_HEAD) for h in hs]

    @pl.when(pl.program_id(1) == 0)
    def _():
        st_ref[...] = jnp.zeros_like(st_ref)

    lb = lb_ref[...]
    f = lb + (1.0 - lb) * _sigmoid(hf_ref[...])
    g = jnp.log(f)
    row = lax.broadcasted_iota(jnp.int32, (ch, ch), 0)
    col = lax.broadcasted_iota(jnp.int32, (ch, ch), 1)
    tri = jnp.where(row >= col, 1.0, 0.0).astype(_BF)
    b = _dot_exact_lhs(tri, g)
    q = _silu(hq_ref[...])
    k = 1.0 - f
    q_s[...] = q
    k_s[...] = k
    b_s[...] = b
    c_s[...] = b - jnp.log(jnp.maximum(k, 0.0))

    st = [st_ref[h] for h in hs]
    qe = q * jnp.exp(b)
    inter = [_bdot(qe[:, sls[h]], st[h], _NT) for h in hs]
    for h in hs:
        o_s[:, sls[h]] = inter[h]

    ones = jnp.ones((_HGRN_HEAD, _LANES), _BF)
    half = sub // 2
    rhalf = lax.broadcasted_iota(jnp.int32, (half, width), 0)
    for i in range(ch // sub):
        r0 = i * sub
        rows = slice(r0, r0 + sub)
        b_i = b_s[rows, :]
        q_i = q_s[rows, :]
        o_i = [o_s[rows, sls[h]] for h in hs]
        if i > 0:
            base = b_s[r0 - 1:r0, :]
            qh = q_i * jnp.exp(b_i - base)
            kh = k_s[0:r0, :] * jnp.exp(base - b_s[0:r0, :])
            sc = [_bdot(qh[:, sls[h]], kh[:, sls[h]], _NT) for h in hs]
            off = [_bdot(sc[h], hi_ref[0:r0, sls[h]]) for h in hs]
            o_i = [o_i[h] + off[h] for h in hs]
        b_lo, b_hi = b_i[:half], b_i[half:]
        q_lo, q_hi = q_i[:half], q_i[half:]
        parts = []
        for s in range(sub):
            crow = c_s[r0 + s:r0 + s + 1, :]
            if s < half:
                p_lo = q_lo * jnp.exp(jnp.minimum(b_lo - crow, 0.0))
                parts.append(jnp.where(rhalf >= s, p_lo, 0.0))
                parts.append(q_hi * jnp.exp(b_hi - crow))
            else:
                p_hi = q_hi * jnp.exp(jnp.minimum(b_hi - crow, 0.0))
                parts.append(jnp.where(rhalf >= s - half, p_hi, 0.0))
        pst = jnp.concatenate(parts, axis=0).astype(_BF)
        sb = [_mm(pst[:, sls[h]], ones) for h in hs]
        o_lo = [o_i[h][:half] for h in hs]
        o_hi = [o_i[h][half:] for h in hs]
        for s in range(sub):
            vrow = hi_ref[r0 + s:r0 + s + 1, :]
            if s < half:
                at = s * sub
                o_lo = [o_lo[h] + sb[h][at:at + half] * vrow[:, sls[h]] for h in hs]
                at += half
            else:
                at = half * sub + (s - half) * half
            o_hi = [o_hi[h] + sb[h][at:at + half] * vrow[:, sls[h]] for h in hs]
        for h in hs:
            o_s[r0:r0 + half, sls[h]] = o_lo[h]
            o_s[r0 + half:r0 + sub, sls[h]] = o_hi[h]

    bl = b_s[ch - 1:ch, :]
    kd = k * jnp.exp(bl - b)
    ebl = jnp.exp(bl)
    upd = [_bdot(hi_ref[:, sls[h]], kd[:, sls[h]], _TN) for h in hs]
    for h in hs:
        st_ref[h] = st[h] * ebl[:, sls[h]] + upd[h]

    gate = _silu(hg_ref[...])
    for h in hs:
        o_ref[:, sls[h]] = _rmsnorm(o_s[:, sls[h]], on_ref[:, sls[h]]) * gate[:, sls[h]]


def _hgrn2(u, lb, onorm, batch, seq):
    t = u.shape[0]
    width = lb.shape[0]
    heads = width // _HGRN_HEAD
    ch = _HGRN_CHUNK
    nc = seq // ch

    def col(off):
        return pl.BlockSpec((ch, width), lambda b, c: (b * nc + c, off))

    vec = pl.BlockSpec((1, width), lambda b, c: (0, 0))
    tile = pltpu.VMEM((ch, width), _F32)
    return pl.pallas_call(
        _hgrn_kernel,
        grid=(batch, nc),
        in_specs=[col(0), col(1), col(2), col(3), vec, vec],
        out_specs=pl.BlockSpec((ch, width), lambda b, c: (b * nc + c, 0)),
        out_shape=jax.ShapeDtypeStruct((t, width), _F32),
        scratch_shapes=[pltpu.VMEM((heads, _HGRN_HEAD, _HGRN_HEAD), _F32)] + [tile] * 5,
        compiler_params=_params("parallel", "arbitrary"),
        name="hgrn2",
    )(u, u, u, u, lb.reshape(1, width), onorm.reshape(1, width))


def _tri_solve(a_list, rhs_list, n, blk):
    row = lax.broadcasted_iota(jnp.int32, (n, n), 0)
    col = lax.broadcasted_iota(jnp.int32, (n, n), 1)
    eye = jnp.where(row == col, 1.0, 0.0)
    shift = int(math.log2(blk))
    same = (row >> shift) == (col >> shift)
    ad = [jnp.where(same, a, 0.0) for a in a_list]
    low = [a - d for a, d in zip(a_list, ad)]
    ks = range(len(a_list))
    x = [eye + d for d in ad]
    p = [_dot3_shared([d], d)[0] for d in ad]
    for _ in range(shift - 2):
        both = [_dot3_shared([x[k], p[k]], p[k]) for k in ks]
        x = [x[k] + both[k][0] for k in ks]
        p = [both[k][1] for k in ks]
    x = [x[k] + _dot3_shared([x[k]], p[k])[0] for k in ks]
    both = [_bdot(x[k], jnp.concatenate([low[k], rhs_list[k]], axis=1)) for k in ks]
    nmat = [both[k][:, :n].astype(_BF) for k in ks]
    term = [both[k][:, n:] for k in ks]
    u = term
    for _ in range((n // 2) // blk - 1):
        term = [_mm(nmat[k], term[k].astype(_BF)) for k in ks]
        u = [u[k] + term[k] for k in ks]
    return u


def _rwkv_kernel(r_ref, k_ref, v_ref, t_ref, mur_ref, muk_ref, muv_ref, mut_ref,
                 w0_ref, a0_ref, kk_ref, ka_ref, rk_ref, lnw_ref, lnb_ref,
                 w2_ref, a2_ref, g2_ref, o_ref,
                 st_ref, pr_ref, pk_ref, pv_ref, pt_ref):
    c = _RWKV_CHUNK
    nbat = r_ref.shape[0]
    pairs = r_ref.shape[2] // _LANES

    @pl.when(pl.program_id(1) == 0)
    def _():
        st_ref[...] = jnp.zeros_like(st_ref)
        pr_ref[...] = jnp.zeros_like(pr_ref)
        pk_ref[...] = jnp.zeros_like(pk_ref)
        pv_ref[...] = jnp.zeros_like(pv_ref)
        pt_ref[...] = jnp.zeros_like(pt_ref)

    def shift_mix(x_ref, prev_ref, mu_ref):
        outs = []
        for bi in range(nbat):
            x = x_ref[bi]
            first = lax.broadcasted_iota(jnp.int32, x.shape, 0) == 0
            xs = jnp.where(first, prev_ref[bi], pltpu.roll(x, 1, axis=0))
            prev_ref[bi] = x_ref[bi, c - 1:c, :]
            outs.append(x + (xs - x) * mu_ref[...])
        return jnp.concatenate(outs, axis=0)

    r = shift_mix(r_ref, pr_ref, mur_ref)
    k = shift_mix(k_ref, pk_ref, muk_ref)
    v = shift_mix(v_ref, pv_ref, muv_ref)
    tl = shift_mix(t_ref, pt_ref, mut_ref)
    lo = tl[:, :_LANES]
    gd = tl[:, _LANES:]

    wpre = w0_ref[...] + _dot3(jnp.tanh(lo), w2_ref[...])
    sp = jnp.maximum(-wpre, 0.0) + jnp.log(1.0 + jnp.exp(-jnp.abs(wpre)))
    logdec = -jnp.exp(-sp - 0.5)
    a = _sigmoid(a0_ref[...] + _dot3(lo, a2_ref[...]))
    g = _dot3(_sigmoid(gd), g2_ref[...])
    kkr = k * kk_ref[...]
    k2 = k * (1.0 + (a - 1.0) * ka_ref[...])
    rkk = r * k2 * rk_ref[...]

    n = 2 * c
    row = lax.broadcasted_iota(jnp.int32, (n, n), 0)
    col = lax.broadcasted_iota(jnp.int32, (n, n), 1)
    strict = row > col
    incl = row >= col
    trow = lax.broadcasted_iota(jnp.int32, (c, c), 0)
    tcol = lax.broadcasted_iota(jnp.int32, (c, c), 1)
    tri = jnp.where(trow >= tcol, 1.0, 0.0).astype(_BF)
    lane = lax.broadcasted_iota(jnp.int32, (1, _LANES), 1)
    head0 = lane < _RWKV_HEAD
    hrow = lax.broadcasted_iota(jnp.int32, (_LANES, _LANES), 0)
    hcol = lax.broadcasted_iota(jnp.int32, (_LANES, _LANES), 1)
    same_head = (hrow < _RWKV_HEAD) == (hcol < _RWKV_HEAD)
    head_ones = jnp.where(same_head, 1.0, 0.0).astype(_BF)

    def stack(x):
        return jnp.concatenate([jnp.where(head0, x, 0.0), jnp.where(head0, 0.0, x)], axis=0)

    def twice(x):
        return jnp.concatenate([x, x], axis=0)

    def unstack(x):
        return jnp.where(head0, x[:c], x[c:])

    units = [(bi, p) for bi in range(nbat) for p in range(pairs)]
    ps = range(len(units))
    sls = [slice(p * _LANES, (p + 1) * _LANES) for _, p in units]

    def cut(x):
        return [x[bi * c:(bi + 1) * c, p * _LANES:(p + 1) * _LANES] for bi, p in units]

    lw = cut(logdec)
    cum = [_dot_exact_lhs(tri, x) for x in lw]
    kk0 = cut(kkr)
    ss = [_dot_exact_rhs(x * x, head_ones) for x in kk0]
    kkp = [x / jnp.maximum(jnp.sqrt(q), 1e-12) for x, q in zip(kk0, ss)]
    ap = cut(a)
    k2p = cut(k2)
    rp = cut(r)
    vp = cut(v)
    gp = cut(g)
    vv = [twice(x) for x in vp]
    ginv = [jnp.exp(-x) for x in cum]
    at = [-kkp[p] * jnp.exp(cum[p] - lw[p]) for p in ps]
    bt = [kkp[p] * ap[p] * ginv[p] for p in ps]
    kt = [k2p[p] * ginv[p] for p in ps]
    rt = [rp[p] * jnp.exp(cum[p]) for p in ps]
    at2 = [stack(x) for x in at]
    rt2 = [stack(x) for x in rt]
    bt2 = [stack(x) for x in bt]
    kt2 = [stack(x) for x in kt]
    gq = [_bdot(jnp.concatenate([at2[p], rt2[p]], axis=0),
                jnp.concatenate([bt2[p], kt2[p]], axis=0), _NT) for p in ps]
    a_ab = [jnp.where(strict, gq[p][:n, :n], 0.0) for p in ps]
    a_ak = [jnp.where(strict, gq[p][:n, n:], 0.0) for p in ps]
    a_rb = [jnp.where(incl, gq[p][n:, :n], 0.0) for p in ps]
    a_rk = [jnp.where(incl, gq[p][n:, n:], 0.0) for p in ps]

    st = [st_ref[p] for p in ps]
    sx = [_bdot(jnp.concatenate([at[p], rt[p]], axis=0), st[p], _NT) for p in ps]
    av = [_bdot(jnp.concatenate([a_ak[p], a_rk[p]], axis=0), vv[p]) for p in ps]
    rhs = [twice(sx[p][:c]) + av[p][:n] for p in ps]
    u2 = _tri_solve(a_ab, rhs, n, _RWKV_BLK)
    uu = [unstack(x) for x in u2]
    y2 = [twice(sx[p][c:]) + _bdot(a_rb[p], twice(uu[p])) + av[p][n:] for p in ps]
    y = [unstack(x) for x in y2]

    cl = [x[c - 1:c, :] for x in cum]
    dl = [jnp.exp(cl[p] - cum[p]) for p in ps]
    upd = [_bdot(jnp.concatenate([uu[p], vp[p]], axis=0),
                 jnp.concatenate([kkp[p] * ap[p] * dl[p], k2p[p] * dl[p]], axis=0), _TN)
           for p in ps]
    for p in ps:
        st_ref[p] = jnp.where(same_head, st[p] * jnp.exp(cl[p]) + upd[p], 0.0)

    inv_n = 1.0 / _RWKV_HEAD
    mu = [_dot_exact_rhs(x, head_ones) * inv_n for x in y]
    yc = [y[p] - mu[p] for p in ps]
    var = [_dot_exact_rhs(x * x, head_ones) * inv_n for x in yc]
    bonus = [_dot_exact_rhs(x, head_ones) for x in cut(rkk)]
    for p in ps:
        sl = sls[p]
        yn = yc[p] * lax.rsqrt(var[p] + _RWKV_LN_EPS) * lnw_ref[:, sl] + lnb_ref[:, sl]
        o_ref[units[p][0], :, sl] = (yn + bonus[p] * vp[p]) * gp[p]


def _rwkv7(u, col0, mu, w0, w2, a0, a2, g2, k_k, k_a, r_k, ln_w, ln_b, batch, seq):
    t = u.shape[0]
    width = w0.shape[0]
    c = _RWKV_CHUNK
    nc = seq // c
    tail = mu.shape[0] - 3 * width
    d_lora = w2.shape[0]
    a_lora = a2.shape[0]
    assert d_lora + a_lora == _LANES and tail - _LANES == g2.shape[0]
    assert col0 % width == 0 and (col0 + 3 * width) % tail == 0
    cb = col0 // width
    w2p = jnp.concatenate([w2, jnp.zeros((a_lora, width), _F32)], axis=0)
    a2p = jnp.concatenate([jnp.zeros((d_lora, width), _F32), a2], axis=0)

    nbat = math.gcd(batch, _RWKV_BATCH_PER_STEP)
    u3 = u.reshape(batch, seq, u.shape[1])

    def colblk(j):
        return pl.BlockSpec((nbat, c, width), lambda b, i: (b, i, cb + j))

    def vec(nn):
        return pl.BlockSpec((1, nn), lambda b, i: (0, 0))

    def mat(m):
        return pl.BlockSpec(m.shape, lambda b, i: (0, 0))

    row = lambda x: x.reshape(1, -1)
    return pl.pallas_call(
        _rwkv_kernel,
        grid=(batch // nbat, nc),
        in_specs=[colblk(0), colblk(1), colblk(2),
                  pl.BlockSpec((nbat, c, tail), lambda b, i: (b, i, (col0 + 3 * width) // tail)),
                  vec(width), vec(width), vec(width), vec(tail)]
        + [vec(width)] * 7 + [mat(w2p), mat(a2p), mat(g2)],
        out_specs=pl.BlockSpec((nbat, c, width), lambda b, i: (b, i, 0)),
        out_shape=jax.ShapeDtypeStruct((batch, seq, width), _F32),
        scratch_shapes=[pltpu.VMEM((nbat * width // _LANES, _LANES, _LANES), _F32),
                        pltpu.VMEM((nbat, 1, width), _F32), pltpu.VMEM((nbat, 1, width), _F32),
                        pltpu.VMEM((nbat, 1, width), _F32), pltpu.VMEM((nbat, 1, tail), _F32)],
        compiler_params=_params("parallel", "arbitrary"),
        name="rwkv7",
    )(u3, u3, u3, u3,
      row(mu[:width]), row(mu[width:2 * width]), row(mu[2 * width:3 * width]), row(mu[3 * width:]),
      row(w0), row(a0), row(k_k), row(k_a), row(r_k), row(ln_w), row(ln_b), w2p, a2p, g2
      ).reshape(t, width)


def _post_kernel(*refs, n_mix):
    x_ref = refs[0]
    o_refs = refs[1:1 + n_mix]
    wo_refs = refs[1 + n_mix:1 + 2 * n_mix]
    (g_ref, wup_ref, wdn_ref, p_ref, wp_ref, pg_ref, wg_ref, out_ref,
     x1_s, h_s, acc_s) = refs[1 + 2 * n_mix:]
    kf = pl.program_id(1)

    @pl.when(kf == 0)
    def _():
        x1 = x_ref[...]
        for o_ref, wo_ref in zip(o_refs, wo_refs):
            x1 = x1 + _mm(o_ref[...].astype(_BF), wo_ref[...])
        x1_s[...] = x1
        h_s[...] = _rmsnorm(x1, g_ref[...]).astype(_BF)
        acc_s[...] = jnp.zeros_like(acc_s)

    act = jnp.square(jnp.maximum(_mm(h_s[...], wup_ref[...]), 0.0))
    acc_s[...] += _mm(act.astype(_BF), wdn_ref[...])

    @pl.when(kf == pl.num_programs(1) - 1)
    def _():
        x2 = x1_s[...] + acc_s[...]
        ple = _rmsnorm(_mm(p_ref[...].astype(_BF), wp_ref[...]), pg_ref[...])
        gate = _sigmoid(_mm(x2.astype(_BF), wg_ref[...]))
        out_ref[...] = x2 + ple * gate


def _post_mixer(x, mix, w_out_parts, g, w_up, w_down, p, w_p, p_g, w_g, tm=512, tf=1024):
    t, d = x.shape
    dff = w_up.shape[1]
    n_mix = len(mix)
    tok = lambda w: pl.BlockSpec((tm, w), lambda i, k: (i, 0))
    full = lambda m: pl.BlockSpec(m.shape, lambda i, k: (0, 0))
    in_specs = ([tok(d)] + [tok(o.shape[1]) for o in mix] + [full(w) for w in w_out_parts]
                + [pl.BlockSpec((1, d), lambda i, k: (0, 0)),
                   pl.BlockSpec((d, tf), lambda i, k: (0, k)),
                   pl.BlockSpec((tf, d), lambda i, k: (k, 0)),
                   tok(p.shape[1]), full(w_p),
                   pl.BlockSpec((1, d), lambda i, k: (0, 0)), full(w_g)])
    return pl.pallas_call(
        functools.partial(_post_kernel, n_mix=n_mix),
        grid=(t // tm, dff // tf),
        in_specs=in_specs,
        out_specs=pl.BlockSpec((tm, d), lambda i, k: (i, 0)),
        out_shape=jax.ShapeDtypeStruct((t, d), _F32),
        scratch_shapes=[pltpu.VMEM((tm, d), _F32), pltpu.VMEM((tm, d), _BF),
                        pltpu.VMEM((tm, d), _F32)],
        compiler_params=_params("parallel", "arbitrary"),
        name="post_mixer",
    )(x, *mix, *w_out_parts, g.reshape(1, d), w_up, w_down, p, w_p, p_g.reshape(1, d), w_g)


def _qk_prep_kernel(x_ref, g_ref, w_ref, cos_ref, sin_ref, qg_ref, kg_ref,
                    q_ref, k_ref, vt_ref, km_ref):
    d = q_ref.shape[1]
    heads = d // _LANES
    j = pl.program_id(1)
    cos = cos_ref[...]
    sin = sin_ref[...]
    qkv = _mm(_rmsnorm(x_ref[...], g_ref[...]).astype(_BF), w_ref[...])

    def rope(x, gain):
        xn = _rmsnorm(x, gain)
        return xn * cos + pltpu.roll(xn, _LANES // 2, axis=1) * sin

    for h in range(heads):
        sl = slice(h * _LANES, (h + 1) * _LANES)
        q_ref[:, sl] = rope(qkv[:, sl], qg_ref[...])
        kr = rope(qkv[:, d + h * _LANES:d + (h + 1) * _LANES], kg_ref[...])
        k_ref[:, sl] = kr.astype(_BF)
        km_ref[0, pl.ds(j, 1), :, sl] = jnp.mean(kr, axis=0, keepdims=True)[None]
    vt_ref[0, 0] = qkv[:, 2 * d:].T.astype(_BF)


def _qk_prep(x, g, w_bf, q_gain, k_gain, batch, seq):
    t, d = x.shape
    blk = _MOBA_BLOCK
    nb = seq // blk
    half = _LANES // 2
    inv_freq = jnp.power(_ROPE_THETA, -jnp.arange(half, dtype=_F32) / half)
    a_in = jnp.arange(blk, dtype=_F32)[None, :, None] * inv_freq
    a_blk = (jnp.arange(nb, dtype=_F32) * blk)[:, None, None] * inv_freq
    cos_h = (jnp.cos(a_blk) * jnp.cos(a_in) - jnp.sin(a_blk) * jnp.sin(a_in)).reshape(seq, half)
    sin_h = (jnp.sin(a_blk) * jnp.cos(a_in) + jnp.cos(a_blk) * jnp.sin(a_in)).reshape(seq, half)
    cos = jnp.concatenate([cos_h, cos_h], axis=1)
    sin = jnp.concatenate([-sin_h, sin_h], axis=1)
    tok = lambda w: pl.BlockSpec((blk, w), lambda b, j: (b * nb + j, 0))
    tab = pl.BlockSpec((blk, _LANES), lambda b, j: (j, 0))
    vec = pl.BlockSpec((1, _LANES), lambda b, j: (0, 0))
    return pl.pallas_call(
        _qk_prep_kernel,
        grid=(batch, nb),
        in_specs=[tok(d), pl.BlockSpec((1, d), lambda b, j: (0, 0)),
                  pl.BlockSpec(w_bf.shape, lambda b, j: (0, 0)), tab, tab, vec, vec],
        out_specs=[tok(d), tok(d), pl.BlockSpec((1, 1, d, blk), lambda b, j: (b, j, 0, 0)),
                   pl.BlockSpec((1, nb, 1, d), lambda b, j: (b, 0, 0, 0))],
        out_shape=[jax.ShapeDtypeStruct((t, d), _F32), jax.ShapeDtypeStruct((t, d), _BF),
                   jax.ShapeDtypeStruct((batch, nb, d, blk), _BF),
                   jax.ShapeDtypeStruct((batch, nb, 1, d), _F32)],
        compiler_params=_params("parallel", "arbitrary"),
        name="qk_prep",
    )(x, g.reshape(1, d), w_bf, cos, sin, q_gain.reshape(1, _LANES), k_gain.reshape(1, _LANES))


def _moba_kernel(q_ref, k_ref, vt_ref, km_ref, o_ref, sel_s, m_s, l_s, acc_s):
    blk = _MOBA_BLOCK
    heads = q_ref.shape[1] // _LANES
    nb = km_ref.shape[1]
    i = pl.program_id(2)
    neg = -jnp.inf
    hs = range(heads)
    sls = [slice(h * _LANES, (h + 1) * _LANES) for h in hs]

    bidx = lax.broadcasted_iota(jnp.int32, (nb, blk), 0)
    qs = [q_ref[:, sls[h]] for h in hs]
    qts = [(qs[h] * (_LANES ** -0.5)).T.astype(_BF) for h in hs]
    gate = [_dot3(km_ref[0, :, sls[h]], qs[h], _NT) for h in hs]
    gate = [jnp.where(bidx < i, gate[h], neg) for h in hs]
    sel = [jnp.zeros((nb, blk), _F32) for h in hs]
    for _ in range(_MOBA_TOPK):
        m = [jnp.max(gate[h], axis=0, keepdims=True) for h in hs]
        cand = [jnp.where((gate[h] == m[h]) & (m[h] > neg), bidx, nb) for h in hs]
        pick = [bidx == jnp.min(cand[h], axis=0, keepdims=True) for h in hs]
        sel = [jnp.where(pick[h], 1.0, sel[h]) for h in hs]
        gate = [jnp.where(pick[h], neg, gate[h]) for h in hs]
    for h in hs:
        sel_s[h] = sel[h]

    ki = lax.broadcasted_iota(jnp.int32, (blk, blk), 0)
    qi = lax.broadcasted_iota(jnp.int32, (blk, blk), 1)
    causal = ki <= qi
    start = pl.multiple_of(i * blk, blk)
    s = [_mm(k_ref[pl.ds(start, blk), sls[h]], qts[h]) for h in hs]
    s = [jnp.where(causal, s[h], neg) for h in hs]
    m = [jnp.max(s[h], axis=0, keepdims=True) for h in hs]
    ones = jnp.ones((16, blk), _BF)
    pr = [jnp.exp((s[h] - m[h]).astype(_BF)) for h in hs]
    pv = [_mm(jnp.concatenate([vt_ref[0, i, sls[h], :], ones], axis=0), pr[h]) for h in hs]
    for h in hs:
        m_s[h] = m[h]
        l_s[h] = pv[h][_LANES:_LANES + 1]
        acc_s[h] = pv[h][:_LANES]

    def past(j, carry):
        kstart = pl.multiple_of(j * blk, blk)
        s = [_mm(k_ref[pl.ds(kstart, blk), sls[h]], qts[h]) for h in hs]
        s = [jnp.where(sel_s[h, pl.ds(j, 1), :] > 0.5, s[h], neg) for h in hs]
        m_old = [m_s[h] for h in hs]
        m_new = [jnp.maximum(m_old[h], jnp.max(s[h], axis=0, keepdims=True)) for h in hs]
        alpha = [jnp.exp(m_old[h] - m_new[h]) for h in hs]
        pr = [jnp.exp((s[h] - m_new[h]).astype(_BF)) for h in hs]
        pv = [_mm(jnp.concatenate([vt_ref[0, j, sls[h], :], ones], axis=0), pr[h]) for h in hs]
        for h in hs:
            m_s[h] = m_new[h]
            l_s[h] = alpha[h] * l_s[h] + pv[h][_LANES:_LANES + 1]
            acc_s[h] = alpha[h] * acc_s[h] + pv[h][:_LANES]
        return carry

    lax.fori_loop(0, i, past, 0)
    for h in hs:
        o_ref[:, sls[h]] = (acc_s[h] / l_s[h]).T


def _moba(q, k, vt, kmean, batch, seq):
    t, d = q.shape
    blk = _MOBA_BLOCK
    nb = seq // blk
    w = _MOBA_HEADS_PER_STEP * _LANES
    qspec = pl.BlockSpec((blk, w), lambda b, h, i: (b * nb + i, h))
    return pl.pallas_call(
        _moba_kernel,
        grid=(batch, d // w, nb),
        in_specs=[qspec,
                  pl.BlockSpec((seq, w), lambda b, h, i: (b, h)),
                  pl.BlockSpec((1, nb, w, blk), lambda b, h, i: (b, 0, h, 0)),
                  pl.BlockSpec((1, nb, w), lambda b, h, i: (b, 0, h))],
        out_specs=qspec,
        out_shape=jax.ShapeDtypeStruct((t, d), _F32),
        scratch_shapes=[pltpu.VMEM((_MOBA_HEADS_PER_STEP, nb, blk), _F32),
                        pltpu.VMEM((_MOBA_HEADS_PER_STEP, 1, blk), _F32),
                        pltpu.VMEM((_MOBA_HEADS_PER_STEP, 1, blk), _F32),
                        pltpu.VMEM((_MOBA_HEADS_PER_STEP, _LANES, blk), _F32)],
        compiler_params=_params("parallel", "parallel", "arbitrary"),
        name="moba_attention",
    )(q, k, vt, kmean)


def kernel(x, p, attn_norm, mlp_norm, w_in_ar, w_out_ar, hgrn_lb, hgrn_onorm, rwkv_mu, rwkv_w0, rwkv_w2, rwkv_a0, rwkv_a2, rwkv_g2, rwkv_kk, rwkv_ka, rwkv_rk, rwkv_ln_w, rwkv_ln_b, w_qkv, w_o_attn, q_norm, k_norm, w_up, w_down, ple_proj, ple_norm, ple_gate):
    batch, seq, d = x.shape
    depth = p.shape[0]
    t = batch * seq
    hw = hgrn_onorm.shape[1]
    bf = lambda w: w.astype(_BF)
    lb_all = jnp.cumsum(jax.nn.softmax(hgrn_lb.astype(_F32), axis=0), axis=0)
    xt = x.reshape(t, d)
    for l in range(depth):
        if l % 2 == 0:
            e = l // 2
            u = _norm_matmul(xt, attn_norm[l], bf(w_in_ar[e]))
            o_a = _hgrn2(u, lb_all[l], hgrn_onorm[e], batch, seq)
            o_b = _rwkv7(u, 4 * hw, rwkv_mu[e], rwkv_w0[e], rwkv_w2[e], rwkv_a0[e], rwkv_a2[e],
                         rwkv_g2[e], rwkv_kk[e], rwkv_ka[e], rwkv_rk[e], rwkv_ln_w[e],
                         rwkv_ln_b[e], batch, seq)
            w_o = bf(w_out_ar[e])
            mix, w_o = [o_a, o_b], [w_o[:hw], w_o[hw:]]
        else:
            o = l // 2
            q, k, vt, kmean = _qk_prep(xt, attn_norm[l], bf(w_qkv[o]), q_norm[o], k_norm[o],
                                       batch, seq)
            kmean = kmean.reshape(batch, -1, d)
            mix, w_o = [_moba(q, k, vt, kmean, batch, seq)], [bf(w_o_attn[o])]
        xt = _post_mixer(xt, mix, w_o, mlp_norm[l], bf(w_up[l]), bf(w_down[l]),
                         p[l].reshape(t, -1), bf(ple_proj[l]), ple_norm[l], bf(ple_gate[l]))
    return xt.reshape(batch, seq, d)
```

```python
import functools
import math

import jax
import jax.numpy as jnp
from jax import lax
from jax.experimental import pallas as pl
from jax.experimental.pallas import tpu as pltpu

_F32 = jnp.float32
_BF = jnp.bfloat16

_NORM_EPS = 1e-6
_RWKV_LN_EPS = 64e-5
_ROPE_THETA = 10000.0

_HGRN_HEAD = 128
_HGRN_CHUNK = 128
_HGRN_SUB = 16
_RWKV_HEAD = 64
_RWKV_CHUNK = 64
_RWKV_BLK = 16
_MOBA_BLOCK = 256
_MOBA_TOPK = 3
_MOBA_HEADS_PER_STEP = 8
_LANES = 128
_VMEM_LIMIT = 56 * 1024 * 1024

_NN = (((1,), (0,)), ((), ()))
_NT = (((1,), (1,)), ((), ()))
_TN = (((0,), (0,)), ((), ()))


def _mm(a, b, dims=_NN):
    return lax.dot_general(a, b, dims, preferred_element_type=_F32)


def _bdot(a, b, dims=_NN):
    return _mm(a.astype(_BF), b.astype(_BF), dims)


def _split2(a):
    hi = a.astype(_BF)
    lo = (a - hi.astype(_F32)).astype(_BF)
    return hi, lo


def _split3(a):
    hi = a.astype(_BF)
    r = a - hi.astype(_F32)
    mid = r.astype(_BF)
    lo = (r - mid.astype(_F32)).astype(_BF)
    return hi, mid, lo


def _dot3(a, b, dims=_NN):
    ah, al = _split2(a)
    bh, bl = _split2(b)
    return _mm(ah, bh, dims) + (_mm(ah, bl, dims) + _mm(al, bh, dims))


def _dot3_shared(lhs, b):
    bh, bl = _split2(b)
    parts = [_split2(a) for a in lhs]
    his = [h for h, _ in parts]
    tot = sum(a.shape[0] for a in lhs)
    r1 = _mm(jnp.concatenate(his + [l for _, l in parts], axis=0), bh)
    r2 = _mm(jnp.concatenate(his, axis=0), bl)
    out, off = [], 0
    for a in lhs:
        m = a.shape[0]
        out.append(r1[off:off + m] + (r1[tot + off:tot + off + m] + r2[off:off + m]))
        off += m
    return out


def _dot_exact_lhs(a_bf, b):
    w = b.shape[1]
    r = _mm(a_bf, jnp.concatenate(_split3(b), axis=1))
    return r[:, :w] + (r[:, w:2 * w] + r[:, 2 * w:])


def _dot_exact_rhs(a, b_bf):
    n = a.shape[0]
    r = _mm(jnp.concatenate(_split3(a), axis=0), b_bf)
    return r[:n] + (r[n:2 * n] + r[2 * n:])


def _rmsnorm(x, g):
    return x * lax.rsqrt(jnp.mean(x * x, axis=-1, keepdims=True) + _NORM_EPS) * g


def _sigmoid(x):
    return 1.0 / (1.0 + jnp.exp(-x))


def _silu(x):
    return x * _sigmoid(x)


def _params(*sem):
    return pltpu.CompilerParams(dimension_semantics=sem, vmem_limit_bytes=_VMEM_LIMIT)


def _norm_matmul_kernel(x_ref, g_ref, w_ref, o_ref):
    h = _rmsnorm(x_ref[...], g_ref[...]).astype(_BF)
    o_ref[...] = _mm(h, w_ref[...])


def _norm_matmul(x, g, w_bf, tm=512):
    t, d = x.shape
    n = w_bf.shape[1]
    return pl.pallas_call(
        _norm_matmul_kernel,
        grid=(t // tm,),
        in_specs=[
            pl.BlockSpec((tm, d), lambda i: (i, 0)),
            pl.BlockSpec((1, d), lambda i: (0, 0)),
            pl.BlockSpec((d, n), lambda i: (0, 0)),
        ],
        out_specs=pl.BlockSpec((tm, n), lambda i: (i, 0)),
        out_shape=jax.ShapeDtypeStruct((t, n), _F32),
        compiler_params=_params("parallel"),
        name="norm_matmul",
    )(x, g.reshape(1, d), w_bf)


def _hgrn_body(hq_ref, hf_ref, hi_ref, hg_ref, lb_ref, on_ref, o_ref,
               st_ref, q_s, k_s, b_s, c_s, o_s, fresh):
    ch, sub = _HGRN_CHUNK, _HGRN_SUB
    width = hq_ref.shape[1]
    hs = range(width // _HGRN_HEAD)
    sls = [slice(h * _HGRN_HEAD, (h + 1) * _HGRN_HEAD) for h in hs]

    lb = lb_ref[...]
    f = lb + (1.0 - lb) * _sigmoid(hf_ref[...])
    g = jnp.log(f)
    row = lax.broadcasted_iota(jnp.int32, (ch, ch), 0)
    col = lax.broadcasted_iota(jnp.int32, (ch, ch), 1)
    tri = jnp.where(row >= col, 1.0, 0.0).astype(_BF)
    b = _dot_exact_lhs(tri, g)
    q = _silu(hq_ref[...])
    k = 1.0 - f
    q_s[...] = q
    k_s[...] = k
    b_s[...] = b
    c_s[...] = b - jnp.log(jnp.maximum(k, 0.0))

    st = [jnp.where(fresh, 0.0, st_ref[h]) for h in hs]
    qe = q * jnp.exp(b)
    inter = [_bdot(qe[:, sls[h]], st[h], _NT) for h in hs]
    for h in hs:
        o_s[:, sls[h]] = inter[h]

    ones = jnp.ones((_HGRN_HEAD, _LANES), _BF)
    half = sub // 2
    rhalf = lax.broadcasted_iota(jnp.int32, (half, width), 0)
    for i in range(ch // sub):
        r0 = i * sub
        rows = slice(r0, r0 + sub)
        b_i = b_s[rows, :]
        q_i = q_s[rows, :]
        o_i = [o_s[rows, sls[h]] for h in hs]
        if i > 0:
            base = b_s[r0 - 1:r0, :]
            qh = q_i * jnp.exp(b_i - base)
            kh = k_s[0:r0, :] * jnp.exp(base - b_s[0:r0, :])
            sc = [_bdot(qh[:, sls[h]], kh[:, sls[h]], _NT) for h in hs]
            off = [_bdot(sc[h], hi_ref[0:r0, sls[h]]) for h in hs]
            o_i = [o_i[h] + off[h] for h in hs]
        b_lo, b_hi = b_i[:half], b_i[half:]
        q_lo, q_hi = q_i[:half], q_i[half:]
        parts = []
        for s in range(sub):
            crow = c_s[r0 + s:r0 + s + 1, :]
            if s < half:
                p_lo = q_lo * jnp.exp(jnp.minimum(b_lo - crow, 0.0))
                parts.append(jnp.where(rhalf >= s, p_lo, 0.0))
                parts.append(q_hi * jnp.exp(b_hi - crow))
            else:
                p_hi = q_hi * jnp.exp(jnp.minimum(b_hi - crow, 0.0))
                parts.append(jnp.where(rhalf >= s - half, p_hi, 0.0))
        pst = jnp.concatenate(parts, axis=0).astype(_BF)
        sb = [_mm(pst[:, sls[h]], ones) for h in hs]
        o_lo = [o_i[h][:half] for h in hs]
        o_hi = [o_i[h][half:] for h in hs]
        for s in range(sub):
            vrow = hi_ref[r0 + s:r0 + s + 1, :]
            if s < half:
                at = s * sub
                o_lo = [o_lo[h] + sb[h][at:at + half] * vrow[:, sls[h]] for h in hs]
                at += half
            else:
                at = half * sub + (s - half) * half
            o_hi = [o_hi[h] + sb[h][at:at + half] * vrow[:, sls[h]] for h in hs]
        for h in hs:
            o_s[r0:r0 + half, sls[h]] = o_lo[h]
            o_s[r0 + half:r0 + sub, sls[h]] = o_hi[h]

    bl = b_s[ch - 1:ch, :]
    kd = k * jnp.exp(bl - b)
    ebl = jnp.exp(bl)
    upd = [_bdot(hi_ref[:, sls[h]], kd[:, sls[h]], _TN) for h in hs]
    for h in hs:
        st_ref[h] = st[h] * ebl[:, sls[h]] + upd[h]

    gate = _silu(hg_ref[...])
    for h in hs:
        o_ref[:, sls[h]] = _rmsnorm(o_s[:, sls[h]], on_ref[:, sls[h]]) * gate[:, sls[h]]


def _tri_solve(a_list, rhs_list, n, blk):
    row = lax.broadcasted_iota(jnp.int32, (n, n), 0)
    col = lax.broadcasted_iota(jnp.int32, (n, n), 1)
    eye = jnp.where(row == col, 1.0, 0.0)
    shift = int(math.log2(blk))
    same = (row >> shift) == (col >> shift)
    ad = [jnp.where(same, a, 0.0) for a in a_list]
    low = [a - d for a, d in zip(a_list, ad)]
    ks = range(len(a_list))
    x = [eye + d for d in ad]
    p = [_dot3_shared([d], d)[0] for d in ad]
    for _ in range(shift - 2):
        both = [_dot3_shared([x[k], p[k]], p[k]) for k in ks]
        x = [x[k] + both[k][0] for k in ks]
        p = [both[k][1] for k in ks]
    x = [x[k] + _dot3_shared([x[k]], p[k])[0] for k in ks]
    both = [_bdot(x[k], jnp.concatenate([low[k], rhs_list[k]], axis=1)) for k in ks]
    nmat = [both[k][:, :n].astype(_BF) for k in ks]
    term = [both[k][:, n:] for k in ks]
    u = term
    for _ in range((n // 2) // blk - 1):
        term = [_mm(nmat[k], term[k].astype(_BF)) for k in ks]
        u = [u[k] + term[k] for k in ks]
    return u


def _rwkv_body(r_ref, k_ref, v_ref, t_ref, mur_ref, muk_ref, muv_ref, mut_ref,
               w0_ref, a0_ref, kk_ref, ka_ref, rk_ref, lnw_ref, lnb_ref,
               w2_ref, a2_ref, g2_ref, o_ref,
               st_ref, pr_ref, pk_ref, pv_ref, pt_ref):
    c = _RWKV_CHUNK
    nbat = r_ref.shape[0]
    pairs = r_ref.shape[2] // _LANES

    def shift_mix(x_ref, prev_ref, mu_ref):
        outs = []
        for bi in range(nbat):
            x = x_ref[bi]
            first = lax.broadcasted_iota(jnp.int32, x.shape, 0) == 0
            xs = jnp.where(first, prev_ref[bi], pltpu.roll(x, 1, axis=0))
            prev_ref[bi] = x_ref[bi, c - 1:c, :]
            outs.append(x + (xs - x) * mu_ref[...])
        return jnp.concatenate(outs, axis=0)

    r = shift_mix(r_ref, pr_ref, mur_ref)
    k = shift_mix(k_ref, pk_ref, muk_ref)
    v = shift_mix(v_ref, pv_ref, muv_ref)
    tl = shift_mix(t_ref, pt_ref, mut_ref)
    lo = tl[:, :_LANES]
    gd = tl[:, _LANES:]

    wpre = w0_ref[...] + _dot3(jnp.tanh(lo), w2_ref[...])
    sp = jnp.maximum(-wpre, 0.0) + jnp.log(1.0 + jnp.exp(-jnp.abs(wpre)))
    logdec = -jnp.exp(-sp - 0.5)
    a = _sigmoid(a0_ref[...] + _dot3(lo, a2_ref[...]))
    g = _dot3(_sigmoid(gd), g2_ref[...])
    kkr = k * kk_ref[...]
    k2 = k * (1.0 + (a - 1.0) * ka_ref[...])
    rkk = r * k2 * rk_ref[...]

    n = 2 * c
    row = lax.broadcasted_iota(jnp.int32, (n, n), 0)
    col = lax.broadcasted_iota(jnp.int32, (n, n), 1)
    strict = row > col
    incl = row >= col
    trow = lax.broadcasted_iota(jnp.int32, (c, c), 0)
    tcol = lax.broadcasted_iota(jnp.int32, (c, c), 1)
    tri = jnp.where(trow >= tcol, 1.0, 0.0).astype(_BF)
    lane = lax.broadcasted_iota(jnp.int32, (1, _LANES), 1)
    head0 = lane < _RWKV_HEAD
    hrow = lax.broadcasted_iota(jnp.int32, (_LANES, _LANES), 0)
    hcol = lax.broadcasted_iota(jnp.int32, (_LANES, _LANES), 1)
    same_head = (hrow < _RWKV_HEAD) == (hcol < _RWKV_HEAD)
    head_ones = jnp.where(same_head, 1.0, 0.0).astype(_BF)

    def stack(x):
        return jnp.concatenate([jnp.where(head0, x, 0.0), jnp.where(head0, 0.0, x)], axis=0)

    def twice(x):
        return jnp.concatenate([x, x], axis=0)

    def unstack(x):
        return jnp.where(head0, x[:c], x[c:])

    units = [(bi, p) for bi in range(nbat) for p in range(pairs)]
    ps = range(len(units))
    sls = [slice(p * _LANES, (p + 1) * _LANES) for _, p in units]

    def cut(x):
        return [x[bi * c:(bi + 1) * c, p * _LANES:(p + 1) * _LANES] for bi, p in units]

    lw = cut(logdec)
    cum = [_dot_exact_lhs(tri, x) for x in lw]
    kk0 = cut(kkr)
    ss = [_dot_exact_rhs(x * x, head_ones) for x in kk0]
    kkp = [x / jnp.maximum(jnp.sqrt(q), 1e-12) for x, q in zip(kk0, ss)]
    ap = cut(a)
    k2p = cut(k2)
    rp = cut(r)
    vp = cut(v)
    gp = cut(g)
    vv = [twice(x) for x in vp]
    ginv = [jnp.exp(-x) for x in cum]
    at = [-kkp[p] * jnp.exp(cum[p] - lw[p]) for p in ps]
    bt = [kkp[p] * ap[p] * ginv[p] for p in ps]
    kt = [k2p[p] * ginv[p] for p in ps]
    rt = [rp[p] * jnp.exp(cum[p]) for p in ps]
    at2 = [stack(x) for x in at]
    rt2 = [stack(x) for x in rt]
    bt2 = [stack(x) for x in bt]
    kt2 = [stack(x) for x in kt]
    gq = [_bdot(jnp.concatenate([at2[p], rt2[p]], axis=0),
                jnp.concatenate([bt2[p], kt2[p]], axis=0), _NT) for p in ps]
    a_ab = [jnp.where(strict, gq[p][:n, :n], 0.0) for p in ps]
    a_ak = [jnp.where(strict, gq[p][:n, n:], 0.0) for p in ps]
    a_rb = [jnp.where(incl, gq[p][n:, :n], 0.0) for p in ps]
    a_rk = [jnp.where(incl, gq[p][n:, n:], 0.0) for p in ps]

    st = [st_ref[p] for p in ps]
    sx = [_bdot(jnp.concatenate([at[p], rt[p]], axis=0), st[p], _NT) for p in ps]
    av = [_bdot(jnp.concatenate([a_ak[p], a_rk[p]], axis=0), vv[p]) for p in ps]
    rhs = [twice(sx[p][:c]) + av[p][:n] for p in ps]
    u2 = _tri_solve(a_ab, rhs, n, _RWKV_BLK)
    uu = [unstack(x) for x in u2]
    y2 = [twice(sx[p][c:]) + _bdot(a_rb[p], twice(uu[p])) + av[p][n:] for p in ps]
    y = [unstack(x) for x in y2]

    cl = [x[c - 1:c, :] for x in cum]
    dl = [jnp.exp(cl[p] - cum[p]) for p in ps]
    upd = [_bdot(jnp.concatenate([uu[p], vp[p]], axis=0),
                 jnp.concatenate([kkp[p] * ap[p] * dl[p], k2p[p] * dl[p]], axis=0), _TN)
           for p in ps]
    for p in ps:
        st_ref[p] = jnp.where(same_head, st[p] * jnp.exp(cl[p]) + upd[p], 0.0)

    inv_n = 1.0 / _RWKV_HEAD
    mu = [_dot_exact_rhs(x, head_ones) * inv_n for x in y]
    yc = [y[p] - mu[p] for p in ps]
    var = [_dot_exact_rhs(x * x, head_ones) * inv_n for x in yc]
    bonus = [_dot_exact_rhs(x, head_ones) for x in cut(rkk)]
    for p in ps:
        sl = sls[p]
        yn = yc[p] * lax.rsqrt(var[p] + _RWKV_LN_EPS) * lnw_ref[:, sl] + lnb_ref[:, sl]
        o_ref[units[p][0], :, sl] = (yn + bonus[p] * vp[p]) * gp[p]


_N_HGRN_IN, _N_RWKV_IN, _N_HGRN_SCRATCH = 6, 18, 6


def _mix0_kernel(*refs, hgrn_chunks, chunks_per_seq):
    a, b = _N_HGRN_IN, _N_HGRN_IN + _N_RWKV_IN
    hgrn_in, rwkv_in = refs[:a], refs[a:b]
    oa_ref, ob_ref = refs[b:b + 2]
    hgrn_sc = refs[b + 2:b + 2 + _N_HGRN_SCRATCH]
    rwkv_sc = refs[b + 2 + _N_HGRN_SCRATCH:]
    i = pl.program_id(0)

    @pl.when(i == 0)
    def _():
        hgrn_sc[0][...] = jnp.zeros_like(hgrn_sc[0])
        for ref in rwkv_sc:
            ref[...] = jnp.zeros_like(ref)

    ch = _HGRN_CHUNK
    for kk in range(hgrn_chunks):
        rows = pl.ds(kk * ch, ch)
        fresh = (i * hgrn_chunks + kk) % chunks_per_seq == 0
        _hgrn_body(*[r.at[rows] for r in hgrn_in[:4]], hgrn_in[4], hgrn_in[5], oa_ref.at[rows],
                   *hgrn_sc, fresh)
        if kk == 0:
            _rwkv_body(*rwkv_in, ob_ref, *rwkv_sc)


def _mix0(u, lb, onorm, mu, w0, w2, a0, a2, g2, k_k, k_a, r_k, ln_w, ln_b, batch, seq):
    t = u.shape[0]
    hw = lb.shape[0]
    width = w0.shape[0]
    c = _RWKV_CHUNK
    nc = seq // c
    ch = _HGRN_CHUNK
    assert (batch * seq) % (nc * ch) == 0 and seq % ch == 0
    hgrn_chunks = batch * seq // (nc * ch)
    col0 = 4 * hw
    tail = mu.shape[0] - 3 * width
    d_lora = w2.shape[0]
    a_lora = a2.shape[0]
    assert d_lora + a_lora == _LANES and tail - _LANES == g2.shape[0]
    assert col0 % width == 0 and (col0 + 3 * width) % tail == 0
    cb = col0 // width
    w2p = jnp.concatenate([w2, jnp.zeros((a_lora, width), _F32)], axis=0)
    a2p = jnp.concatenate([jnp.zeros((d_lora, width), _F32), a2], axis=0)
    u3 = u.reshape(batch, seq, u.shape[1])
    hrows = hgrn_chunks * ch

    def hcol(off):
        return pl.BlockSpec((hrows, hw), lambda i: (i, off))

    def colblk(j):
        return pl.BlockSpec((batch, c, width), lambda i: (0, i, cb + j))

    def vec(nn):
        return pl.BlockSpec((1, nn), lambda i: (0, 0))

    def mat(m):
        return pl.BlockSpec(m.shape, lambda i: (0, 0))

    row = lambda x: x.reshape(1, -1)
    tile = pltpu.VMEM((ch, hw), _F32)
    o_a, o_b = pl.pallas_call(
        functools.partial(_mix0_kernel, hgrn_chunks=hgrn_chunks, chunks_per_seq=seq // ch),
        grid=(nc,),
        in_specs=[hcol(0), hcol(1), hcol(2), hcol(3), vec(hw), vec(hw),
                  colblk(0), colblk(1), colblk(2),
                  pl.BlockSpec((batch, c, tail), lambda i: (0, i, (col0 + 3 * width) // tail)),
                  vec(width), vec(width), vec(width), vec(tail)]
        + [vec(width)] * 7 + [mat(w2p), mat(a2p), mat(g2)],
        out_specs=[pl.BlockSpec((hrows, hw), lambda i: (i, 0)),
                   pl.BlockSpec((batch, c, width), lambda i: (0, i, 0))],
        out_shape=[jax.ShapeDtypeStruct((t, hw), _F32),
                   jax.ShapeDtypeStruct((batch, seq, width), _F32)],
        scratch_shapes=[pltpu.VMEM((hw // _HGRN_HEAD, _HGRN_HEAD, _HGRN_HEAD), _F32)] + [tile] * 5
        + [pltpu.VMEM((batch * width // _LANES, _LANES, _LANES), _F32),
           pltpu.VMEM((batch, 1, width), _F32), pltpu.VMEM((batch, 1, width), _F32),
           pltpu.VMEM((batch, 1, width), _F32), pltpu.VMEM((batch, 1, tail), _F32)],
        compiler_params=_params("arbitrary"),
        name="mix0",
    )(u, u, u, u, row(lb), row(onorm), u3, u3, u3, u3,
      row(mu[:width]), row(mu[width:2 * width]), row(mu[2 * width:3 * width]), row(mu[3 * width:]),
      row(w0), row(a0), row(k_k), row(k_a), row(r_k), row(ln_w), row(ln_b), w2p, a2p, g2)
    return o_a, o_b.reshape(t, width)


def _post_kernel(*refs, n_mix):
    x_ref = refs[0]
    o_refs = refs[1:1 + n_mix]
    wo_refs = refs[1 + n_mix:1 + 2 * n_mix]
    (g_ref, wup_ref, wdn_ref, p_ref, wp_ref, pg_ref, wg_ref, out_ref,
     x1_s, h_s, acc_s) = refs[1 + 2 * n_mix:]
    kf = pl.program_id(1)

    @pl.when(kf == 0)
    def _():
        x1 = x_ref[...]
        for o_ref, wo_ref in zip(o_refs, wo_refs):
            x1 = x1 + _mm(o_ref[...].astype(_BF), wo_ref[...])
        x1_s[...] = x1
        h_s[...] = _rmsnorm(x1, g_ref[...]).astype(_BF)
        acc_s[...] = jnp.zeros_like(acc_s)

    act = jnp.square(jnp.maximum(_mm(h_s[...], wup_ref[...]), 0.0))
    acc_s[...] += _mm(act.astype(_BF), wdn_ref[...])

    @pl.when(kf == pl.num_programs(1) - 1)
    def _():
        x2 = x1_s[...] + acc_s[...]
        ple = _rmsnorm(_mm(p_ref[...].astype(_BF), wp_ref[...]), pg_ref[...])
        gate = _sigmoid(_mm(x2.astype(_BF), wg_ref[...]))
        out_ref[...] = x2 + ple * gate


def _post_mixer(x, mix, w_out_parts, g, w_up, w_down, p, w_p, p_g, w_g, tm=512, tf=1024):
    t, d = x.shape
    dff = w_up.shape[1]
    n_mix = len(mix)
    tok = lambda w: pl.BlockSpec((tm, w), lambda i, k: (i, 0))
    full = lambda m: pl.BlockSpec(m.shape, lambda i, k: (0, 0))
    in_specs = ([tok(d)] + [tok(o.shape[1]) for o in mix] + [full(w) for w in w_out_parts]
                + [pl.BlockSpec((1, d), lambda i, k: (0, 0)),
                   pl.BlockSpec((d, tf), lambda i, k: (0, k)),
                   pl.BlockSpec((tf, d), lambda i, k: (k, 0)),
                   tok(p.shape[1]), full(w_p),
                   pl.BlockSpec((1, d), lambda i, k: (0, 0)), full(w_g)])
    return pl.pallas_call(
        functools.partial(_post_kernel, n_mix=n_mix),
        grid=(t // tm, dff // tf),
        in_specs=in_specs,
        out_specs=pl.BlockSpec((tm, d), lambda i, k: (i, 0)),
        out_shape=jax.ShapeDtypeStruct((t, d), _F32),
        scratch_shapes=[pltpu.VMEM((tm, d), _F32), pltpu.VMEM((tm, d), _BF),
                        pltpu.VMEM((tm, d), _F32)],
        compiler_params=_params("parallel", "arbitrary"),
        name="post_mixer",
    )(x, *mix, *w_out_parts, g.reshape(1, d), w_up, w_down, p, w_p, p_g.reshape(1, d), w_g)


def _qk_prep_kernel(x_ref, g_ref, w_ref, cos_ref, sin_ref, qg_ref, kg_ref,
                    q_ref, k_ref, vt_ref, km_ref):
    d = q_ref.shape[1]
    heads = d // _LANES
    j = pl.program_id(1)
    cos = cos_ref[...]
    sin = sin_ref[...]
    qkv = _mm(_rmsnorm(x_ref[...], g_ref[...]).astype(_BF), w_ref[...])

    def rope(x, gain):
        xn = _rmsnorm(x, gain)
        return xn * cos + pltpu.roll(xn, _LANES // 2, axis=1) * sin

    for h in range(heads):
        sl = slice(h * _LANES, (h + 1) * _LANES)
        q_ref[:, sl] = rope(qkv[:, sl], qg_ref[...])
        kr = rope(qkv[:, d + h * _LANES:d + (h + 1) * _LANES], kg_ref[...])
        k_ref[:, sl] = kr.astype(_BF)
        km_ref[0, pl.ds(j, 1), :, sl] = jnp.mean(kr, axis=0, keepdims=True)[None]
    vt_ref[0, 0] = qkv[:, 2 * d:].T.astype(_BF)


def _qk_prep(x, g, w_bf, q_gain, k_gain, batch, seq):
    t, d = x.shape
    blk = _MOBA_BLOCK
    nb = seq // blk
    half = _LANES // 2
    inv_freq = jnp.power(_ROPE_THETA, -jnp.arange(half, dtype=_F32) / half)
    a_in = jnp.arange(blk, dtype=_F32)[None, :, None] * inv_freq
    a_blk = (jnp.arange(nb, dtype=_F32) * blk)[:, None, None] * inv_freq
    cos_h = (jnp.cos(a_blk) * jnp.cos(a_in) - jnp.sin(a_blk) * jnp.sin(a_in)).reshape(seq, half)
    sin_h = (jnp.sin(a_blk) * jnp.cos(a_in) + jnp.cos(a_blk) * jnp.sin(a_in)).reshape(seq, half)
    cos = jnp.concatenate([cos_h, cos_h], axis=1)
    sin = jnp.concatenate([-sin_h, sin_h], axis=1)
    tok = lambda w: pl.BlockSpec((blk, w), lambda b, j: (b * nb + j, 0))
    tab = pl.BlockSpec((blk, _LANES), lambda b, j: (j, 0))
    vec = pl.BlockSpec((1, _LANES), lambda b, j: (0, 0))
    return pl.pallas_call(
        _qk_prep_kernel,
        grid=(batch, nb),
        in_specs=[tok(d), pl.BlockSpec((1, d), lambda b, j: (0, 0)),
                  pl.BlockSpec(w_bf.shape, lambda b, j: (0, 0)), tab, tab, vec, vec],
        out_specs=[tok(d), tok(d), pl.BlockSpec((1, 1, d, blk), lambda b, j: (b, j, 0, 0)),
                   pl.BlockSpec((1, nb, 1, d), lambda b, j: (b, 0, 0, 0))],
        out_shape=[jax.ShapeDtypeStruct((t, d), _F32), jax.ShapeDtypeStruct((t, d), _BF),
                   jax.ShapeDtypeStruct((batch, nb, d, blk), _BF),
                   jax.ShapeDtypeStruct((batch, nb, 1, d), _F32)],
        compiler_params=_params("parallel", "arbitrary"),
        name="qk_prep",
    )(x, g.reshape(1, d), w_bf, cos, sin, q_gain.reshape(1, _LANES), k_gain.reshape(1, _LANES))


def _moba_kernel(q_ref, k_ref, vt_ref, km_ref, o_ref, sel_s, m_s, l_s, acc_s):
    blk = _MOBA_BLOCK
    heads = q_ref.shape[1] // _LANES
    nb = km_ref.shape[1]
    i = pl.program_id(2)
    neg = -jnp.inf
    hs = range(heads)
    sls = [slice(h * _LANES, (h + 1) * _LANES) for h in hs]

    bidx = lax.broadcasted_iota(jnp.int32, (nb, blk), 0)
    qs = [q_ref[:, sls[h]] for h in hs]
    qts = [(qs[h] * (_LANES ** -0.5)).T.astype(_BF) for h in hs]
    gate = [_dot3(km_ref[0, :, sls[h]], qs[h], _NT) for h in hs]
    gate = [jnp.where(bidx < i, gate[h], neg) for h in hs]
    sel = [jnp.zeros((nb, blk), _F32) for h in hs]
    for _ in range(_MOBA_TOPK):
        m = [jnp.max(gate[h], axis=0, keepdims=True) for h in hs]
        cand = [jnp.where((gate[h] == m[h]) & (m[h] > neg), bidx, nb) for h in hs]
        pick = [bidx == jnp.min(cand[h], axis=0, keepdims=True) for h in hs]
        sel = [jnp.where(pick[h], 1.0, sel[h]) for h in hs]
        gate = [jnp.where(pick[h], neg, gate[h]) for h in hs]
    for h in hs:
        sel_s[h] = sel[h]

    ki = lax.broadcasted_iota(jnp.int32, (blk, blk), 0)
    qi = lax.broadcasted_iota(jnp.int32, (blk, blk), 1)
    causal = ki <= qi
    start = pl.multiple_of(i * blk, blk)
    s = [_mm(k_ref[pl.ds(start, blk), sls[h]], qts[h]) for h in hs]
    s = [jnp.where(causal, s[h], neg) for h in hs]
    m = [jnp.max(s[h], axis=0, keepdims=True) for h in hs]
    ones = jnp.ones((16, blk), _BF)
    pr = [jnp.exp((s[h] - m[h]).astype(_BF)) for h in hs]
    pv = [_mm(jnp.concatenate([vt_ref[0, i, sls[h], :], ones], axis=0), pr[h]) for h in hs]
    for h in hs:
        m_s[h] = m[h]
        l_s[h] = pv[h][_LANES:_LANES + 1]
        acc_s[h] = pv[h][:_LANES]

    def past(j, carry):
        kstart = pl.multiple_of(j * blk, blk)
        s = [_mm(k_ref[pl.ds(kstart, blk), sls[h]], qts[h]) for h in hs]
        s = [jnp.where(sel_s[h, pl.ds(j, 1), :] > 0.5, s[h], neg) for h in hs]
        m_old = [m_s[h] for h in hs]
        m_new = [jnp.maximum(m_old[h], jnp.max(s[h], axis=0, keepdims=True)) for h in hs]
        alpha = [jnp.exp(m_old[h] - m_new[h]) for h in hs]
        pr = [jnp.exp((s[h] - m_new[h]).astype(_BF)) for h in hs]
        pv = [_mm(jnp.concatenate([vt_ref[0, j, sls[h], :], ones], axis=0), pr[h]) for h in hs]
        for h in hs:
            m_s[h] = m_new[h]
            l_s[h] = alpha[h] * l_s[h] + pv[h][_LANES:_LANES + 1]
            acc_s[h] = alpha[h] * acc_s[h] + pv[h][:_LANES]
        return carry

    lax.fori_loop(0, i, past, 0)
    for h in hs:
        o_ref[:, sls[h]] = (acc_s[h] / l_s[h]).T


def _moba(q, k, vt, kmean, batch, seq):
    t, d = q.shape
    blk = _MOBA_BLOCK
    nb = seq // blk
    w = _MOBA_HEADS_PER_STEP * _LANES
    qspec = pl.BlockSpec((blk, w), lambda b, h, i: (b * nb + i, h))
    return pl.pallas_call(
        _moba_kernel,
        grid=(batch, d // w, nb),
        in_specs=[qspec,
                  pl.BlockSpec((seq, w), lambda b, h, i: (b, h)),
                  pl.BlockSpec((1, nb, w, blk), lambda b, h, i: (b, 0, h, 0)),
                  pl.BlockSpec((1, nb, w), lambda b, h, i: (b, 0, h))],
        out_specs=qspec,
        out_shape=jax.ShapeDtypeStruct((t, d), _F32),
        scratch_shapes=[pltpu.VMEM((_MOBA_HEADS_PER_STEP, nb, blk), _F32),
                        pltpu.VMEM((_MOBA_HEADS_PER_STEP, 1, blk), _F32),
                        pltpu.VMEM((_MOBA_HEADS_PER_STEP, 1, blk), _F32),
                        pltpu.VMEM((_MOBA_HEADS_PER_STEP, _LANES, blk), _F32)],
        compiler_params=_params("parallel", "parallel", "arbitrary"),
        name="moba_attention",
    )(q, k, vt, kmean)


def kernel(x, p, attn_norm, mlp_norm, w_in_ar, w_out_ar, hgrn_lb, hgrn_onorm, rwkv_mu, rwkv_w0, rwkv_w2, rwkv_a0, rwkv_a2, rwkv_g2, rwkv_kk, rwkv_ka, rwkv_rk, rwkv_ln_w, rwkv_ln_b, w_qkv, w_o_attn, q_norm, k_norm, w_up, w_down, ple_proj, ple_norm, ple_gate):
    batch, seq, d = x.shape
    depth = p.shape[0]
    t = batch * seq
    hw = hgrn_onorm.shape[1]
    bf = lambda w: w.astype(_BF)
    lb_all = jnp.cumsum(jax.nn.softmax(hgrn_lb.astype(_F32), axis=0), axis=0)
    xt = x.reshape(t, d)
    for l in range(depth):
        if l % 2 == 0:
            e = l // 2
            u = _norm_matmul(xt, attn_norm[l], bf(w_in_ar[e]))
            o_a, o_b = _mix0(u, lb_all[l], hgrn_onorm[e], rwkv_mu[e], rwkv_w0[e], rwkv_w2[e],
                             rwkv_a0[e], rwkv_a2[e], rwkv_g2[e], rwkv_kk[e], rwkv_ka[e],
                             rwkv_rk[e], rwkv_ln_w[e], rwkv_ln_b[e], batch, seq)
            w_o = bf(w_out_ar[e])
            mix, w_o = [o_a, o_b], [w_o[:hw], w_o[hw:]]
        else:
            o = l // 2
            q, k, vt, kmean = _qk_prep(xt, attn_norm[l], bf(w_qkv[o]), q_norm[o], k_norm[o],
                                       batch, seq)
            kmean = kmean.reshape(batch, -1, d)
            mix, w_o = [_moba(q, k, vt, kmean, batch, seq)], [bf(w_o_attn[o])]
        xt = _post_mixer(xt, mix, w_o, mlp_norm[l], bf(w_up[l]), bf(w_down[l]),
                         p[l].reshape(t, -1), bf(ple_proj[l]), ple_norm[l], bf(ple_gate[l]))
    return xt.reshape(batch, seq, d)
```

```python
import functools
import math

import jax
import jax.numpy as jnp
from jax import lax
from jax.experimental import pallas as pl
from jax.experimental.pallas import tpu as pltpu

_F32 = jnp.float32
_BF = jnp.bfloat16

_NORM_EPS = 1e-6
_RWKV_LN_EPS = 64e-5
_ROPE_THETA = 10000.0

_HGRN_HEAD = 128
_HGRN_CHUNK = 128
_HGRN_SUB = 16
_RWKV_HEAD = 64
_RWKV_CHUNK = 64
_RWKV_BLK = 16
_MOBA_BLOCK = 256
_MOBA_TOPK = 3
_MOBA_HEADS_PER_STEP = 8
_LANES = 128
_VMEM_LIMIT = 56 * 1024 * 1024

_NN = (((1,), (0,)), ((), ()))
_NT = (((1,), (1,)), ((), ()))
_TN = (((0,), (0,)), ((), ()))


def _mm(a, b, dims=_NN):
    return lax.dot_general(a, b, dims, preferred_element_type=_F32)


def _bdot(a, b, dims=_NN):
    return _mm(a.astype(_BF), b.astype(_BF), dims)


def _split2(a):
    hi = a.astype(_BF)
    lo = (a - hi.astype(_F32)).astype(_BF)
    return hi, lo


def _split3(a):
    hi = a.astype(_BF)
    r = a - hi.astype(_F32)
    mid = r.astype(_BF)
    lo = (r - mid.astype(_F32)).astype(_BF)
    return hi, mid, lo


def _dot3(a, b, dims=_NN):
    ah, al = _split2(a)
    bh, bl = _split2(b)
    return _mm(ah, bh, dims) + (_mm(ah, bl, dims) + _mm(al, bh, dims))


def _dot3_shared(lhs, b):
    bh, bl = _split2(b)
    parts = [_split2(a) for a in lhs]
    his = [h for h, _ in parts]
    tot = sum(a.shape[0] for a in lhs)
    r1 = _mm(jnp.concatenate(his + [l for _, l in parts], axis=0), bh)
    r2 = _mm(jnp.concatenate(his, axis=0), bl)
    out, off = [], 0
    for a in lhs:
        m = a.shape[0]
        out.append(r1[off:off + m] + (r1[tot + off:tot + off + m] + r2[off:off + m]))
        off += m
    return out


def _dot_exact_lhs(a_bf, b):
    w = b.shape[1]
    r = _mm(a_bf, jnp.concatenate(_split3(b), axis=1))
    return r[:, :w] + (r[:, w:2 * w] + r[:, 2 * w:])


def _dot_exact_rhs(a, b_bf):
    n = a.shape[0]
    r = _mm(jnp.concatenate(_split3(a), axis=0), b_bf)
    return r[:n] + (r[n:2 * n] + r[2 * n:])


def _rmsnorm(x, g):
    return x * lax.rsqrt(jnp.mean(x * x, axis=-1, keepdims=True) + _NORM_EPS) * g


def _sigmoid(x):
    return 1.0 / (1.0 + jnp.exp(-x))


def _silu(x):
    return x * _sigmoid(x)


def _params(*sem):
    return pltpu.CompilerParams(dimension_semantics=sem, vmem_limit_bytes=_VMEM_LIMIT)


def _norm_matmul_kernel(x_ref, g_ref, w_ref, o_ref):
    h = _rmsnorm(x_ref[...], g_ref[...]).astype(_BF)
    o_ref[...] = _mm(h, w_ref[...])


def _norm_matmul(x, g, w_bf, tm=512):
    t, d = x.shape
    n = w_bf.shape[1]
    return pl.pallas_call(
        _norm_matmul_kernel,
        grid=(t // tm,),
        in_specs=[
            pl.BlockSpec((tm, d), lambda i: (i, 0)),
            pl.BlockSpec((1, d), lambda i: (0, 0)),
            pl.BlockSpec((d, n), lambda i: (0, 0)),
        ],
        out_specs=pl.BlockSpec((tm, n), lambda i: (i, 0)),
        out_shape=jax.ShapeDtypeStruct((t, n), _F32),
        compiler_params=_params("parallel"),
        name="norm_matmul",
    )(x, g.reshape(1, d), w_bf)


def _hgrn_body(hq_ref, hf_ref, hi_ref, hg_ref, lb_ref, on_ref, o_ref,
               st_ref, q_s, k_s, b_s, c_s, o_s, fresh):
    ch, sub = _HGRN_CHUNK, _HGRN_SUB
    width = hq_ref.shape[1]
    hs = range(width // _HGRN_HEAD)
    sls = [slice(h * _HGRN_HEAD, (h + 1) * _HGRN_HEAD) for h in hs]

    lb = lb_ref[...]
    f = lb + (1.0 - lb) * _sigmoid(hf_ref[...])
    g = jnp.log(f)
    row = lax.broadcasted_iota(jnp.int32, (ch, ch), 0)
    col = lax.broadcasted_iota(jnp.int32, (ch, ch), 1)
    tri = jnp.where(row >= col, 1.0, 0.0).astype(_BF)
    b = _dot_exact_lhs(tri, g)
    q = _silu(hq_ref[...])
    k = 1.0 - f
    q_s[...] = q
    k_s[...] = k
    b_s[...] = b
    c_s[...] = b - jnp.log(jnp.maximum(k, 0.0))

    st = [jnp.where(fresh, 0.0, st_ref[h]) for h in hs]
    qe = q * jnp.exp(b)
    inter = [_bdot(qe[:, sls[h]], st[h], _NT) for h in hs]
    for h in hs:
        o_s[:, sls[h]] = inter[h]
    yield

    ones = jnp.ones((_HGRN_HEAD, _LANES), _BF)
    half = sub // 2
    rhalf = lax.broadcasted_iota(jnp.int32, (half, width), 0)
    for i in range(ch // sub):
        r0 = i * sub
        rows = slice(r0, r0 + sub)
        b_i = b_s[rows, :]
        q_i = q_s[rows, :]
        o_i = [o_s[rows, sls[h]] for h in hs]
        if i > 0:
            base = b_s[r0 - 1:r0, :]
            qh = q_i * jnp.exp(b_i - base)
            kh = k_s[0:r0, :] * jnp.exp(base - b_s[0:r0, :])
            sc = [_bdot(qh[:, sls[h]], kh[:, sls[h]], _NT) for h in hs]
            off = [_bdot(sc[h], hi_ref[0:r0, sls[h]]) for h in hs]
            o_i = [o_i[h] + off[h] for h in hs]
        b_lo, b_hi = b_i[:half], b_i[half:]
        q_lo, q_hi = q_i[:half], q_i[half:]
        parts = []
        for s in range(sub):
            crow = c_s[r0 + s:r0 + s + 1, :]
            if s < half:
                p_lo = q_lo * jnp.exp(jnp.minimum(b_lo - crow, 0.0))
                parts.append(jnp.where(rhalf >= s, p_lo, 0.0))
                parts.append(q_hi * jnp.exp(b_hi - crow))
            else:
                p_hi = q_hi * jnp.exp(jnp.minimum(b_hi - crow, 0.0))
                parts.append(jnp.where(rhalf >= s - half, p_hi, 0.0))
        pst = jnp.concatenate(parts, axis=0).astype(_BF)
        sb = [_mm(pst[:, sls[h]], ones) for h in hs]
        o_lo = [o_i[h][:half] for h in hs]
        o_hi = [o_i[h][half:] for h in hs]
        for s in range(sub):
            vrow = hi_ref[r0 + s:r0 + s + 1, :]
            if s < half:
                at = s * sub
                o_lo = [o_lo[h] + sb[h][at:at + half] * vrow[:, sls[h]] for h in hs]
                at += half
            else:
                at = half * sub + (s - half) * half
            o_hi = [o_hi[h] + sb[h][at:at + half] * vrow[:, sls[h]] for h in hs]
        for h in hs:
            o_s[r0:r0 + half, sls[h]] = o_lo[h]
            o_s[r0 + half:r0 + sub, sls[h]] = o_hi[h]
        yield

    bl = b_s[ch - 1:ch, :]
    kd = k * jnp.exp(bl - b)
    ebl = jnp.exp(bl)
    upd = [_bdot(hi_ref[:, sls[h]], kd[:, sls[h]], _TN) for h in hs]
    for h in hs:
        st_ref[h] = st[h] * ebl[:, sls[h]] + upd[h]

    gate = _silu(hg_ref[...])
    for h in hs:
        o_ref[:, sls[h]] = _rmsnorm(o_s[:, sls[h]], on_ref[:, sls[h]]) * gate[:, sls[h]]


def _tri_solve(a_list, rhs_list, n, blk):
    row = lax.broadcasted_iota(jnp.int32, (n, n), 0)
    col = lax.broadcasted_iota(jnp.int32, (n, n), 1)
    eye = jnp.where(row == col, 1.0, 0.0)
    shift = int(math.log2(blk))
    same = (row >> shift) == (col >> shift)
    ad = [jnp.where(same, a, 0.0) for a in a_list]
    low = [a - d for a, d in zip(a_list, ad)]
    ks = range(len(a_list))
    x = [eye + d for d in ad]
    p = [_dot3_shared([d], d)[0] for d in ad]
    yield
    for _ in range(shift - 2):
        both = [_dot3_shared([x[k], p[k]], p[k]) for k in ks]
        x = [x[k] + both[k][0] for k in ks]
        p = [both[k][1] for k in ks]
        yield
    x = [x[k] + _dot3_shared([x[k]], p[k])[0] for k in ks]
    yield
    both = [_bdot(x[k], jnp.concatenate([low[k], rhs_list[k]], axis=1)) for k in ks]
    nmat = [both[k][:, :n].astype(_BF) for k in ks]
    term = [both[k][:, n:] for k in ks]
    u = term
    yield
    for _ in range((n // 2) // blk - 1):
        term = [_mm(nmat[k], term[k].astype(_BF)) for k in ks]
        u = [u[k] + term[k] for k in ks]
    return u


def _rwkv_body(r_ref, k_ref, v_ref, t_ref, mur_ref, muk_ref, muv_ref, mut_ref,
               w0_ref, a0_ref, kk_ref, ka_ref, rk_ref, lnw_ref, lnb_ref,
               w2_ref, a2_ref, g2_ref, o_ref,
               st_ref, pr_ref, pk_ref, pv_ref, pt_ref):
    c = _RWKV_CHUNK
    nbat = r_ref.shape[0]
    pairs = r_ref.shape[2] // _LANES

    def shift_mix(x_ref, prev_ref, mu_ref):
        outs = []
        for bi in range(nbat):
            x = x_ref[bi]
            first = lax.broadcasted_iota(jnp.int32, x.shape, 0) == 0
            xs = jnp.where(first, prev_ref[bi], pltpu.roll(x, 1, axis=0))
            prev_ref[bi] = x_ref[bi, c - 1:c, :]
            outs.append(x + (xs - x) * mu_ref[...])
        return jnp.concatenate(outs, axis=0)

    r = shift_mix(r_ref, pr_ref, mur_ref)
    k = shift_mix(k_ref, pk_ref, muk_ref)
    v = shift_mix(v_ref, pv_ref, muv_ref)
    tl = shift_mix(t_ref, pt_ref, mut_ref)
    lo = tl[:, :_LANES]
    gd = tl[:, _LANES:]

    wpre = w0_ref[...] + _dot3(jnp.tanh(lo), w2_ref[...])
    sp = jnp.maximum(-wpre, 0.0) + jnp.log(1.0 + jnp.exp(-jnp.abs(wpre)))
    logdec = -jnp.exp(-sp - 0.5)
    a = _sigmoid(a0_ref[...] + _dot3(lo, a2_ref[...]))
    g = _dot3(_sigmoid(gd), g2_ref[...])
    kkr = k * kk_ref[...]
    k2 = k * (1.0 + (a - 1.0) * ka_ref[...])
    rkk = r * k2 * rk_ref[...]
    yield

    n = 2 * c
    row = lax.broadcasted_iota(jnp.int32, (n, n), 0)
    col = lax.broadcasted_iota(jnp.int32, (n, n), 1)
    strict = row > col
    incl = row >= col
    trow = lax.broadcasted_iota(jnp.int32, (c, c), 0)
    tcol = lax.broadcasted_iota(jnp.int32, (c, c), 1)
    tri = jnp.where(trow >= tcol, 1.0, 0.0).astype(_BF)
    lane = lax.broadcasted_iota(jnp.int32, (1, _LANES), 1)
    head0 = lane < _RWKV_HEAD
    hrow = lax.broadcasted_iota(jnp.int32, (_LANES, _LANES), 0)
    hcol = lax.broadcasted_iota(jnp.int32, (_LANES, _LANES), 1)
    same_head = (hrow < _RWKV_HEAD) == (hcol < _RWKV_HEAD)
    head_ones = jnp.where(same_head, 1.0, 0.0).astype(_BF)

    def stack(x):
        return jnp.concatenate([jnp.where(head0, x, 0.0), jnp.where(head0, 0.0, x)], axis=0)

    def twice(x):
        return jnp.concatenate([x, x], axis=0)

    def unstack(x):
        return jnp.where(head0, x[:c], x[c:])

    units = [(bi, p) for bi in range(nbat) for p in range(pairs)]
    ps = range(len(units))
    sls = [slice(p * _LANES, (p + 1) * _LANES) for _, p in units]

    def cut(x):
        return [x[bi * c:(bi + 1) * c, p * _LANES:(p + 1) * _LANES] for bi, p in units]

    lw = cut(logdec)
    cum = [_dot_exact_lhs(tri, x) for x in lw]
    kk0 = cut(kkr)
    ss = [_dot_exact_rhs(x * x, head_ones) for x in kk0]
    kkp = [x / jnp.maximum(jnp.sqrt(q), 1e-12) for x, q in zip(kk0, ss)]
    ap = cut(a)
    k2p = cut(k2)
    rp = cut(r)
    vp = cut(v)
    gp = cut(g)
    vv = [twice(x) for x in vp]
    ginv = [jnp.exp(-x) for x in cum]
    at = [-kkp[p] * jnp.exp(cum[p] - lw[p]) for p in ps]
    bt = [kkp[p] * ap[p] * ginv[p] for p in ps]
    kt = [k2p[p] * ginv[p] for p in ps]
    rt = [rp[p] * jnp.exp(cum[p]) for p in ps]
    at2 = [stack(x) for x in at]
    rt2 = [stack(x) for x in rt]
    bt2 = [stack(x) for x in bt]
    kt2 = [stack(x) for x in kt]
    yield
    gq = [_bdot(jnp.concatenate([at2[p], rt2[p]], axis=0),
                jnp.concatenate([bt2[p], kt2[p]], axis=0), _NT) for p in ps]
    a_ab = [jnp.where(strict, gq[p][:n, :n], 0.0) for p in ps]
    a_ak = [jnp.where(strict, gq[p][:n, n:], 0.0) for p in ps]
    a_rb = [jnp.where(incl, gq[p][n:, :n], 0.0) for p in ps]
    a_rk = [jnp.where(incl, gq[p][n:, n:], 0.0) for p in ps]

    st = [st_ref[p] for p in ps]
    sx = [_bdot(jnp.concatenate([at[p], rt[p]], axis=0), st[p], _NT) for p in ps]
    av = [_bdot(jnp.concatenate([a_ak[p], a_rk[p]], axis=0), vv[p]) for p in ps]
    rhs = [twice(sx[p][:c]) + av[p][:n] for p in ps]
    yield
    u2 = yield from _tri_solve(a_ab, rhs, n, _RWKV_BLK)
    uu = [unstack(x) for x in u2]
    y2 = [twice(sx[p][c:]) + _bdot(a_rb[p], twice(uu[p])) + av[p][n:] for p in ps]
    y = [unstack(x) for x in y2]
    yield

    cl = [x[c - 1:c, :] for x in cum]
    dl = [jnp.exp(cl[p] - cum[p]) for p in ps]
    upd = [_bdot(jnp.concatenate([uu[p], vp[p]], axis=0),
                 jnp.concatenate([kkp[p] * ap[p] * dl[p], k2p[p] * dl[p]], axis=0), _TN)
           for p in ps]
    for p in ps:
        st_ref[p] = jnp.where(same_head, st[p] * jnp.exp(cl[p]) + upd[p], 0.0)
    yield

    inv_n = 1.0 / _RWKV_HEAD
    mu = [_dot_exact_rhs(x, head_ones) * inv_n for x in y]
    yc = [y[p] - mu[p] for p in ps]
    var = [_dot_exact_rhs(x * x, head_ones) * inv_n for x in yc]
    bonus = [_dot_exact_rhs(x, head_ones) for x in cut(rkk)]
    for p in ps:
        sl = sls[p]
        yn = yc[p] * lax.rsqrt(var[p] + _RWKV_LN_EPS) * lnw_ref[:, sl] + lnb_ref[:, sl]
        o_ref[units[p][0], :, sl] = (yn + bonus[p] * vp[p]) * gp[p]


_N_HGRN_IN, _N_RWKV_IN, _N_HGRN_SCRATCH = 6, 18, 6


def _mix0_kernel(*refs, hgrn_chunks, chunks_per_seq):
    a, b = _N_HGRN_IN, _N_HGRN_IN + _N_RWKV_IN
    hgrn_in, rwkv_in = refs[:a], refs[a:b]
    oa_ref, ob_ref = refs[b:b + 2]
    hgrn_sc = refs[b + 2:b + 2 + _N_HGRN_SCRATCH]
    rwkv_sc = refs[b + 2 + _N_HGRN_SCRATCH:]
    i = pl.program_id(0)

    @pl.when(i == 0)
    def _():
        hgrn_sc[0][...] = jnp.zeros_like(hgrn_sc[0])
        for ref in rwkv_sc:
            ref[...] = jnp.zeros_like(ref)

    ch = _HGRN_CHUNK

    def hgrn_all():
        for kk in range(hgrn_chunks):
            rows = pl.ds(kk * ch, ch)
            fresh = (i * hgrn_chunks + kk) % chunks_per_seq == 0
            yield from _hgrn_body(*[r.at[rows] for r in hgrn_in[:4]], hgrn_in[4], hgrn_in[5],
                                  oa_ref.at[rows], *hgrn_sc, fresh)

    live = [hgrn_all(), _rwkv_body(*rwkv_in, ob_ref, *rwkv_sc)]
    while live:
        for gen in list(live):
            if next(gen, live) is live:
                live.remove(gen)


def _mix0(u, lb, onorm, mu, w0, w2, a0, a2, g2, k_k, k_a, r_k, ln_w, ln_b, batch, seq):
    t = u.shape[0]
    hw = lb.shape[0]
    width = w0.shape[0]
    c = _RWKV_CHUNK
    nc = seq // c
    ch = _HGRN_CHUNK
    assert (batch * seq) % (nc * ch) == 0 and seq % ch == 0
    hgrn_chunks = batch * seq // (nc * ch)
    col0 = 4 * hw
    tail = mu.shape[0] - 3 * width
    d_lora = w2.shape[0]
    a_lora = a2.shape[0]
    assert d_lora + a_lora == _LANES and tail - _LANES == g2.shape[0]
    assert col0 % width == 0 and (col0 + 3 * width) % tail == 0
    cb = col0 // width
    w2p = jnp.concatenate([w2, jnp.zeros((a_lora, width), _F32)], axis=0)
    a2p = jnp.concatenate([jnp.zeros((d_lora, width), _F32), a2], axis=0)
    u3 = u.reshape(batch, seq, u.shape[1])
    hrows = hgrn_chunks * ch

    def hcol(off):
        return pl.BlockSpec((hrows, hw), lambda i: (i, off))

    def colblk(j):
        return pl.BlockSpec((batch, c, width), lambda i: (0, i, cb + j))

    def vec(nn):
        return pl.BlockSpec((1, nn), lambda i: (0, 0))

    def mat(m):
        return pl.BlockSpec(m.shape, lambda i: (0, 0))

    row = lambda x: x.reshape(1, -1)
    tile = pltpu.VMEM((ch, hw), _F32)
    o_a, o_b = pl.pallas_call(
        functools.partial(_mix0_kernel, hgrn_chunks=hgrn_chunks, chunks_per_seq=seq // ch),
        grid=(nc,),
        in_specs=[hcol(0), hcol(1), hcol(2), hcol(3), vec(hw), vec(hw),
                  colblk(0), colblk(1), colblk(2),
                  pl.BlockSpec((batch, c, tail), lambda i: (0, i, (col0 + 3 * width) // tail)),
                  vec(width), vec(width), vec(width), vec(tail)]
        + [vec(width)] * 7 + [mat(w2p), mat(a2p), mat(g2)],
        out_specs=[pl.BlockSpec((hrows, hw), lambda i: (i, 0)),
                   pl.BlockSpec((batch, c, width), lambda i: (0, i, 0))],
        out_shape=[jax.ShapeDtypeStruct((t, hw), _F32),
                   jax.ShapeDtypeStruct((batch, seq, width), _F32)],
        scratch_shapes=[pltpu.VMEM((hw // _HGRN_HEAD, _HGRN_HEAD, _HGRN_HEAD), _F32)] + [tile] * 5
        + [pltpu.VMEM((batch * width // _LANES, _LANES, _LANES), _F32),
           pltpu.VMEM((batch, 1, width), _F32), pltpu.VMEM((batch, 1, width), _F32),
           pltpu.VMEM((batch, 1, width), _F32), pltpu.VMEM((batch, 1, tail), _F32)],
        compiler_params=_params("arbitrary"),
        name="mix0",
    )(u, u, u, u, row(lb), row(onorm), u3, u3, u3, u3,
      row(mu[:width]), row(mu[width:2 * width]), row(mu[2 * width:3 * width]), row(mu[3 * width:]),
      row(w0), row(a0), row(k_k), row(k_a), row(r_k), row(ln_w), row(ln_b), w2p, a2p, g2)
    return o_a, o_b.reshape(t, width)


def _post_kernel(*refs, n_mix):
    x_ref = refs[0]
    o_refs = refs[1:1 + n_mix]
    wo_refs = refs[1 + n_mix:1 + 2 * n_mix]
    (g_ref, wup_ref, wdn_ref, p_ref, wp_ref, pg_ref, wg_ref, out_ref,
     x1_s, h_s, acc_s) = refs[1 + 2 * n_mix:]
    kf = pl.program_id(1)

    @pl.when(kf == 0)
    def _():
        x1 = x_ref[...]
        for o_ref, wo_ref in zip(o_refs, wo_refs):
            x1 = x1 + _mm(o_ref[...].astype(_BF), wo_ref[...])
        x1_s[...] = x1
        h_s[...] = _rmsnorm(x1, g_ref[...]).astype(_BF)
        acc_s[...] = jnp.zeros_like(acc_s)

    act = jnp.square(jnp.maximum(_mm(h_s[...], wup_ref[...]), 0.0))
    acc_s[...] += _mm(act.astype(_BF), wdn_ref[...])

    @pl.when(kf == pl.num_programs(1) - 1)
    def _():
        x2 = x1_s[...] + acc_s[...]
        ple = _rmsnorm(_mm(p_ref[...].astype(_BF), wp_ref[...]), pg_ref[...])
        gate = _sigmoid(_mm(x2.astype(_BF), wg_ref[...]))
        out_ref[...] = x2 + ple * gate


def _post_mixer(x, mix, w_out_parts, g, w_up, w_down, p, w_p, p_g, w_g, tm=512, tf=1024):
    t, d = x.shape
    dff = w_up.shape[1]
    n_mix = len(mix)
    tok = lambda w: pl.BlockSpec((tm, w), lambda i, k: (i, 0))
    full = lambda m: pl.BlockSpec(m.shape, lambda i, k: (0, 0))
    in_specs = ([tok(d)] + [tok(o.shape[1]) for o in mix] + [full(w) for w in w_out_parts]
                + [pl.BlockSpec((1, d), lambda i, k: (0, 0)),
                   pl.BlockSpec((d, tf), lambda i, k: (0, k)),
                   pl.BlockSpec((tf, d), lambda i, k: (k, 0)),
                   tok(p.shape[1]), full(w_p),
                   pl.BlockSpec((1, d), lambda i, k: (0, 0)), full(w_g)])
    return pl.pallas_call(
        functools.partial(_post_kernel, n_mix=n_mix),
        grid=(t // tm, dff // tf),
        in_specs=in_specs,
        out_specs=pl.BlockSpec((tm, d), lambda i, k: (i, 0)),
        out_shape=jax.ShapeDtypeStruct((t, d), _F32),
        scratch_shapes=[pltpu.VMEM((tm, d), _F32), pltpu.VMEM((tm, d), _BF),
                        pltpu.VMEM((tm, d), _F32)],
        compiler_params=_params("parallel", "arbitrary"),
        name="post_mixer",
    )(x, *mix, *w_out_parts, g.reshape(1, d), w_up, w_down, p, w_p, p_g.reshape(1, d), w_g)


def _qk_prep_kernel(x_ref, g_ref, w_ref, cos_ref, sin_ref, qg_ref, kg_ref,
                    q_ref, k_ref, vt_ref, km_ref):
    d = q_ref.shape[1]
    heads = d // _LANES
    j = pl.program_id(1)
    cos = cos_ref[...]
    sin = sin_ref[...]
    qkv = _mm(_rmsnorm(x_ref[...], g_ref[...]).astype(_BF), w_ref[...])

    def rope(x, gain):
        xn = _rmsnorm(x, gain)
        return xn * cos + pltpu.roll(xn, _LANES // 2, axis=1) * sin

    for h in range(heads):
        sl = slice(h * _LANES, (h + 1) * _LANES)
        q_ref[:, sl] = rope(qkv[:, sl], qg_ref[...])
        kr = rope(qkv[:, d + h * _LANES:d + (h + 1) * _LANES], kg_ref[...])
        k_ref[:, sl] = kr.astype(_BF)
        km_ref[0, pl.ds(j, 1), :, sl] = jnp.mean(kr, axis=0, keepdims=True)[None]
    vt_ref[0, 0] = qkv[:, 2 * d:].T.astype(_BF)


def _qk_prep(x, g, w_bf, q_gain, k_gain, batch, seq):
    t, d = x.shape
    blk = _MOBA_BLOCK
    nb = seq // blk
    half = _LANES // 2
    inv_freq = jnp.power(_ROPE_THETA, -jnp.arange(half, dtype=_F32) / half)
    a_in = jnp.arange(blk, dtype=_F32)[None, :, None] * inv_freq
    a_blk = (jnp.arange(nb, dtype=_F32) * blk)[:, None, None] * inv_freq
    cos_h = (jnp.cos(a_blk) * jnp.cos(a_in) - jnp.sin(a_blk) * jnp.sin(a_in)).reshape(seq, half)
    sin_h = (jnp.sin(a_blk) * jnp.cos(a_in) + jnp.cos(a_blk) * jnp.sin(a_in)).reshape(seq, half)
    cos = jnp.concatenate([cos_h, cos_h], axis=1)
    sin = jnp.concatenate([-sin_h, sin_h], axis=1)
    tok = lambda w: pl.BlockSpec((blk, w), lambda b, j: (b * nb + j, 0))
    tab = pl.BlockSpec((blk, _LANES), lambda b, j: (j, 0))
    vec = pl.BlockSpec((1, _LANES), lambda b, j: (0, 0))
    return pl.pallas_call(
        _qk_prep_kernel,
        grid=(batch, nb),
        in_specs=[tok(d), pl.BlockSpec((1, d), lambda b, j: (0, 0)),
                  pl.BlockSpec(w_bf.shape, lambda b, j: (0, 0)), tab, tab, vec, vec],
        out_specs=[tok(d), tok(d), pl.BlockSpec((1, 1, d, blk), lambda b, j: (b, j, 0, 0)),
                   pl.BlockSpec((1, nb, 1, d), lambda b, j: (b, 0, 0, 0))],
        out_shape=[jax.ShapeDtypeStruct((t, d), _F32), jax.ShapeDtypeStruct((t, d), _BF),
                   jax.ShapeDtypeStruct((batch, nb, d, blk), _BF),
                   jax.ShapeDtypeStruct((batch, nb, 1, d), _F32)],
        compiler_params=_params("parallel", "arbitrary"),
        name="qk_prep",
    )(x, g.reshape(1, d), w_bf, cos, sin, q_gain.reshape(1, _LANES), k_gain.reshape(1, _LANES))


def _moba_kernel(q_ref, k_ref, vt_ref, km_ref, o_ref, sel_s, m_s, l_s, acc_s):
    blk = _MOBA_BLOCK
    heads = q_ref.shape[1] // _LANES
    nb = km_ref.shape[1]
    i = pl.program_id(2)
    neg = -jnp.inf
    hs = range(heads)
    sls = [slice(h * _LANES, (h + 1) * _LANES) for h in hs]

    bidx = lax.broadcasted_iota(jnp.int32, (nb, blk), 0)
    qs = [q_ref[:, sls[h]] for h in hs]
    qts = [(qs[h] * (_LANES ** -0.5)).T.astype(_BF) for h in hs]
    gate = [_dot3(km_ref[0, :, sls[h]], qs[h], _NT) for h in hs]
    gate = [jnp.where(bidx < i, gate[h], neg) for h in hs]
    sel = [jnp.zeros((nb, blk), _F32) for h in hs]
    for _ in range(_MOBA_TOPK):
        m = [jnp.max(gate[h], axis=0, keepdims=True) for h in hs]
        cand = [jnp.where((gate[h] == m[h]) & (m[h] > neg), bidx, nb) for h in hs]
        pick = [bidx == jnp.min(cand[h], axis=0, keepdims=True) for h in hs]
        sel = [jnp.where(pick[h], 1.0, sel[h]) for h in hs]
        gate = [jnp.where(pick[h], neg, gate[h]) for h in hs]
    for h in hs:
        sel_s[h] = sel[h]

    ki = lax.broadcasted_iota(jnp.int32, (blk, blk), 0)
    qi = lax.broadcasted_iota(jnp.int32, (blk, blk), 1)
    causal = ki <= qi
    start = pl.multiple_of(i * blk, blk)
    s = [_mm(k_ref[pl.ds(start, blk), sls[h]], qts[h]) for h in hs]
    s = [jnp.where(causal, s[h], neg) for h in hs]
    m = [jnp.max(s[h], axis=0, keepdims=True).astype(_BF).astype(_F32) for h in hs]
    ones = jnp.ones((16, blk), _BF)
    pr = [jnp.exp((s[h] - m[h]).astype(_BF)) for h in hs]
    pv = [_mm(jnp.concatenate([vt_ref[0, i, sls[h], :], ones], axis=0), pr[h]) for h in hs]
    for h in hs:
        m_s[h] = m[h]
        l_s[h] = pv[h][_LANES:_LANES + 1]
        acc_s[h] = pv[h][:_LANES]

    def past(j, carry):
        kstart = pl.multiple_of(j * blk, blk)
        s = [_mm(k_ref[pl.ds(kstart, blk), sls[h]], qts[h]).astype(_BF) for h in hs]
        s = [jnp.where(sel_s[h, pl.ds(j, 1), :] > 0.5, s[h], neg) for h in hs]
        m_old = [m_s[h] for h in hs]
        m_new = [jnp.maximum(m_old[h], jnp.max(s[h], axis=0, keepdims=True).astype(_F32))
                 for h in hs]
        alpha = [jnp.exp(m_old[h] - m_new[h]) for h in hs]
        pr = [jnp.exp(s[h] - m_new[h].astype(_BF)) for h in hs]
        pv = [_mm(jnp.concatenate([vt_ref[0, j, sls[h], :], ones], axis=0), pr[h]) for h in hs]
        for h in hs:
            m_s[h] = m_new[h]
            l_s[h] = alpha[h] * l_s[h] + pv[h][_LANES:_LANES + 1]
            acc_s[h] = alpha[h] * acc_s[h] + pv[h][:_LANES]
        return carry

    lax.fori_loop(0, i, past, 0)
    for h in hs:
        o_ref[:, sls[h]] = (acc_s[h] / l_s[h]).T


def _moba(q, k, vt, kmean, batch, seq):
    t, d = q.shape
    blk = _MOBA_BLOCK
    nb = seq // blk
    w = _MOBA_HEADS_PER_STEP * _LANES
    qspec = pl.BlockSpec((blk, w), lambda b, h, i: (b * nb + i, h))
    return pl.pallas_call(
        _moba_kernel,
        grid=(batch, d // w, nb),
        in_specs=[qspec,
                  pl.BlockSpec((seq, w), lambda b, h, i: (b, h)),
                  pl.BlockSpec((1, nb, w, blk), lambda b, h, i: (b, 0, h, 0)),
                  pl.BlockSpec((1, nb, w), lambda b, h, i: (b, 0, h))],
        out_specs=qspec,
        out_shape=jax.ShapeDtypeStruct((t, d), _F32),
        scratch_shapes=[pltpu.VMEM((_MOBA_HEADS_PER_STEP, nb, blk), _F32),
                        pltpu.VMEM((_MOBA_HEADS_PER_STEP, 1, blk), _F32),
                        pltpu.VMEM((_MOBA_HEADS_PER_STEP, 1, blk), _F32),
                        pltpu.VMEM((_MOBA_HEADS_PER_STEP, _LANES, blk), _F32)],
        compiler_params=_params("parallel", "parallel", "arbitrary"),
        name="moba_attention",
    )(q, k, vt, kmean)


def kernel(x, p, attn_norm, mlp_norm, w_in_ar, w_out_ar, hgrn_lb, hgrn_onorm, rwkv_mu, rwkv_w0, rwkv_w2, rwkv_a0, rwkv_a2, rwkv_g2, rwkv_kk, rwkv_ka, rwkv_rk, rwkv_ln_w, rwkv_ln_b, w_qkv, w_o_attn, q_norm, k_norm, w_up, w_down, ple_proj, ple_norm, ple_gate):
    batch, seq, d = x.shape
    depth = p.shape[0]
    t = batch * seq
    hw = hgrn_onorm.shape[1]
    bf = lambda w: w.astype(_BF)
    lb_all = jnp.cumsum(jax.nn.softmax(hgrn_lb.astype(_F32), axis=0), axis=0)
    xt = x.reshape(t, d)
    for l in range(depth):
        if l % 2 == 0:
            e = l // 2
            u = _norm_matmul(xt, attn_norm[l], bf(w_in_ar[e]))
            o_a, o_b = _mix0(u, lb_all[l], hgrn_onorm[e], rwkv_mu[e], rwkv_w0[e], rwkv_w2[e],
                             rwkv_a0[e], rwkv_a2[e], rwkv_g2[e], rwkv_kk[e], rwkv_ka[e],
                             rwkv_rk[e], rwkv_ln_w[e], rwkv_ln_b[e], batch, seq)
            w_o = bf(w_out_ar[e])
            mix, w_o = [o_a, o_b], [w_o[:hw], w_o[hw:]]
        else:
            o = l // 2
            q, k, vt, kmean = _qk_prep(xt, attn_norm[l], bf(w_qkv[o]), q_norm[o], k_norm[o],
                                       batch, seq)
            kmean = kmean.reshape(batch, -1, d)
            mix, w_o = [_moba(q, k, vt, kmean, batch, seq)], [bf(w_o_attn[o])]
        xt = _post_mixer(xt, mix, w_o, mlp_norm[l], bf(w_up[l]), bf(w_down[l]),
                         p[l].reshape(t, -1), bf(ple_proj[l]), ple_norm[l], bf(ple_gate[l]))
    return xt.reshape(batch, seq, d)
```

```python
import functools
import math

import jax
import jax.numpy as jnp
from jax import lax
from jax.experimental import pallas as pl
from jax.experimental.pallas import tpu as pltpu

_F32 = jnp.float32
_BF = jnp.bfloat16

_NORM_EPS = 1e-6
_RWKV_LN_EPS = 64e-5
_ROPE_THETA = 10000.0

_HGRN_HEAD = 128
_HGRN_CHUNK = 128
_HGRN_SUB = 16
_RWKV_HEAD = 64
_RWKV_CHUNK = 64
_RWKV_BLK = 16
_MOBA_BLOCK = 256
_MOBA_TOPK = 3
_MOBA_HEADS_PER_STEP = 8
_LANES = 128
_VMEM_LIMIT = 56 * 1024 * 1024

_NN = (((1,), (0,)), ((), ()))
_NT = (((1,), (1,)), ((), ()))
_TN = (((0,), (0,)), ((), ()))


def _mm(a, b, dims=_NN):
    return lax.dot_general(a, b, dims, preferred_element_type=_F32)


def _bdot(a, b, dims=_NN):
    return _mm(a.astype(_BF), b.astype(_BF), dims)


def _split2(a):
    hi = a.astype(_BF)
    lo = (a - hi.astype(_F32)).astype(_BF)
    return hi, lo


def _split3(a):
    hi = a.astype(_BF)
    r = a - hi.astype(_F32)
    mid = r.astype(_BF)
    lo = (r - mid.astype(_F32)).astype(_BF)
    return hi, mid, lo


def _dot3(a, b, dims=_NN):
    ah, al = _split2(a)
    bh, bl = _split2(b)
    return _mm(ah, bh, dims) + (_mm(ah, bl, dims) + _mm(al, bh, dims))


def _dot3_shared(lhs, b):
    bh, bl = _split2(b)
    parts = [_split2(a) for a in lhs]
    his = [h for h, _ in parts]
    tot = sum(a.shape[0] for a in lhs)
    r1 = _mm(jnp.concatenate(his + [l for _, l in parts], axis=0), bh)
    r2 = _mm(jnp.concatenate(his, axis=0), bl)
    out, off = [], 0
    for a in lhs:
        m = a.shape[0]
        out.append(r1[off:off + m] + (r1[tot + off:tot + off + m] + r2[off:off + m]))
        off += m
    return out


def _dot_exact_lhs(a_bf, b):
    w = b.shape[1]
    r = _mm(a_bf, jnp.concatenate(_split3(b), axis=1))
    return r[:, :w] + (r[:, w:2 * w] + r[:, 2 * w:])


def _dot_hilo_rhs(a, b_bf):
    n = a.shape[0]
    r = _mm(jnp.concatenate(_split2(a), axis=0), b_bf)
    return r[:n] + r[n:]


def _rmsnorm(x, g):
    return x * lax.rsqrt(jnp.mean(x * x, axis=-1, keepdims=True) + _NORM_EPS) * g


def _sigmoid(x):
    return 1.0 / (1.0 + jnp.exp(-x))


def _silu(x):
    return x * _sigmoid(x)


def _params(*sem):
    return pltpu.CompilerParams(dimension_semantics=sem, vmem_limit_bytes=_VMEM_LIMIT)


def _norm_matmul_kernel(x_ref, g_ref, w_ref, o_ref):
    h = _rmsnorm(x_ref[...], g_ref[...]).astype(_BF)
    o_ref[...] = _mm(h, w_ref[...])


def _norm_matmul(x, g, w_bf, tm=512):
    t, d = x.shape
    n = w_bf.shape[1]
    return pl.pallas_call(
        _norm_matmul_kernel,
        grid=(t // tm,),
        in_specs=[
            pl.BlockSpec((tm, d), lambda i: (i, 0)),
            pl.BlockSpec((1, d), lambda i: (0, 0)),
            pl.BlockSpec((d, n), lambda i: (0, 0)),
        ],
        out_specs=pl.BlockSpec((tm, n), lambda i: (i, 0)),
        out_shape=jax.ShapeDtypeStruct((t, n), _F32),
        compiler_params=_params("parallel"),
        name="norm_matmul",
    )(x, g.reshape(1, d), w_bf)


def _hgrn_body(hq_ref, hf_ref, hi_ref, hg_ref, lb_ref, on_ref, o_ref,
               st_ref, q_s, k_s, b_s, c_s, o_s, fresh):
    ch, sub = _HGRN_CHUNK, _HGRN_SUB
    width = hq_ref.shape[1]
    hs = range(width // _HGRN_HEAD)
    sls = [slice(h * _HGRN_HEAD, (h + 1) * _HGRN_HEAD) for h in hs]

    lb = lb_ref[...]
    f = lb + (1.0 - lb) * _sigmoid(hf_ref[...])
    g = jnp.log(f)
    row = lax.broadcasted_iota(jnp.int32, (ch, ch), 0)
    col = lax.broadcasted_iota(jnp.int32, (ch, ch), 1)
    tri = jnp.where(row >= col, 1.0, 0.0).astype(_BF)
    b = _dot_exact_lhs(tri, g)
    q = _silu(hq_ref[...])
    k = 1.0 - f
    q_s[...] = q
    k_s[...] = k
    b_s[...] = b
    c_s[...] = b - jnp.log(jnp.maximum(k, 0.0))

    st = [jnp.where(fresh, 0.0, st_ref[h]) for h in hs]
    qe = q * jnp.exp(b)
    inter = [_bdot(qe[:, sls[h]], st[h], _NT) for h in hs]
    for h in hs:
        o_s[:, sls[h]] = inter[h]

    ones = jnp.ones((_HGRN_HEAD, _LANES), _BF)
    half = sub // 2
    rhalf = lax.broadcasted_iota(jnp.int32, (half, width), 0)
    for i in range(ch // sub):
        r0 = i * sub
        rows = slice(r0, r0 + sub)
        b_i = b_s[rows, :]
        q_i = q_s[rows, :]
        o_i = [o_s[rows, sls[h]] for h in hs]
        if i > 0:
            base = b_s[r0 - 1:r0, :]
            qh = q_i * jnp.exp(b_i - base)
            kh = k_s[0:r0, :] * jnp.exp(base - b_s[0:r0, :])
            sc = [_bdot(qh[:, sls[h]], kh[:, sls[h]], _NT) for h in hs]
            off = [_bdot(sc[h], hi_ref[0:r0, sls[h]]) for h in hs]
            o_i = [o_i[h] + off[h] for h in hs]
        b_lo, b_hi = b_i[:half], b_i[half:]
        q_lo, q_hi = q_i[:half], q_i[half:]
        parts = []
        for s in range(sub):
            crow = c_s[r0 + s:r0 + s + 1, :]
            if s < half:
                p_lo = q_lo * jnp.exp(jnp.minimum(b_lo - crow, 0.0))
                parts.append(jnp.where(rhalf >= s, p_lo, 0.0))
                parts.append(q_hi * jnp.exp(b_hi - crow))
            else:
                p_hi = q_hi * jnp.exp(jnp.minimum(b_hi - crow, 0.0))
                parts.append(jnp.where(rhalf >= s - half, p_hi, 0.0))
        pst = jnp.concatenate(parts, axis=0).astype(_BF)
        sb = [_mm(pst[:, sls[h]], ones) for h in hs]
        o_lo = [o_i[h][:half] for h in hs]
        o_hi = [o_i[h][half:] for h in hs]
        for s in range(sub):
            vrow = hi_ref[r0 + s:r0 + s + 1, :]
            if s < half:
                at = s * sub
                o_lo = [o_lo[h] + sb[h][at:at + half] * vrow[:, sls[h]] for h in hs]
                at += half
            else:
                at = half * sub + (s - half) * half
            o_hi = [o_hi[h] + sb[h][at:at + half] * vrow[:, sls[h]] for h in hs]
        for h in hs:
            o_s[r0:r0 + half, sls[h]] = o_lo[h]
            o_s[r0 + half:r0 + sub, sls[h]] = o_hi[h]

    bl = b_s[ch - 1:ch, :]
    kd = k * jnp.exp(bl - b)
    ebl = jnp.exp(bl)
    upd = [_bdot(hi_ref[:, sls[h]], kd[:, sls[h]], _TN) for h in hs]
    for h in hs:
        st_ref[h] = st[h] * ebl[:, sls[h]] + upd[h]

    gate = _silu(hg_ref[...])
    for h in hs:
        o_ref[:, sls[h]] = _rmsnorm(o_s[:, sls[h]], on_ref[:, sls[h]]) * gate[:, sls[h]]


def _tri_solve(a_list, rhs_list, n, blk):
    row = lax.broadcasted_iota(jnp.int32, (n, n), 0)
    col = lax.broadcasted_iota(jnp.int32, (n, n), 1)
    eye = jnp.where(row == col, 1.0, 0.0)
    shift = int(math.log2(blk))
    same = (row >> shift) == (col >> shift)
    ad = [jnp.where(same, a, 0.0) for a in a_list]
    low = [a - d for a, d in zip(a_list, ad)]
    ks = range(len(a_list))
    x = [eye + d for d in ad]
    p = [_dot3_shared([d], d)[0] for d in ad]
    for _ in range(shift - 2):
        both = [_dot3_shared([x[k], p[k]], p[k]) for k in ks]
        x = [x[k] + both[k][0] for k in ks]
        p = [both[k][1] for k in ks]
    x = [x[k] + _dot3_shared([x[k]], p[k])[0] for k in ks]
    both = [_bdot(x[k], jnp.concatenate([low[k], rhs_list[k]], axis=1)) for k in ks]
    nmat = [both[k][:, :n].astype(_BF) for k in ks]
    term = [both[k][:, n:] for k in ks]
    u = term
    for _ in range((n // 2) // blk - 1):
        term = [_mm(nmat[k], term[k].astype(_BF)) for k in ks]
        u = [u[k] + term[k] for k in ks]
    return u


def _rwkv_body(r_ref, k_ref, v_ref, t_ref, mur_ref, muk_ref, muv_ref, mut_ref,
               w0_ref, a0_ref, kk_ref, ka_ref, rk_ref, lnw_ref, lnb_ref,
               w2_ref, a2_ref, g2_ref, o_ref,
               st_ref, pr_ref, pk_ref, pv_ref, pt_ref):
    c = _RWKV_CHUNK
    nbat = r_ref.shape[0]
    pairs = r_ref.shape[2] // _LANES

    def shift_mix(x_ref, prev_ref, mu_ref):
        outs = []
        for bi in range(nbat):
            x = x_ref[bi]
            first = lax.broadcasted_iota(jnp.int32, x.shape, 0) == 0
            xs = jnp.where(first, prev_ref[bi], pltpu.roll(x, 1, axis=0))
            prev_ref[bi] = x_ref[bi, c - 1:c, :]
            outs.append(x + (xs - x) * mu_ref[...])
        return jnp.concatenate(outs, axis=0)

    r = shift_mix(r_ref, pr_ref, mur_ref)
    k = shift_mix(k_ref, pk_ref, muk_ref)
    v = shift_mix(v_ref, pv_ref, muv_ref)
    tl = shift_mix(t_ref, pt_ref, mut_ref)
    lo = tl[:, :_LANES]
    gd = tl[:, _LANES:]

    wpre = w0_ref[...] + _dot3(jnp.tanh(lo), w2_ref[...])
    sp = jnp.maximum(-wpre, 0.0) + jnp.log(1.0 + jnp.exp(-jnp.abs(wpre)))
    logdec = -jnp.exp(-sp - 0.5)
    a = _sigmoid(a0_ref[...] + _dot3(lo, a2_ref[...]))
    g = _dot3(_sigmoid(gd), g2_ref[...])
    kkr = k * kk_ref[...]
    k2 = k * (1.0 + (a - 1.0) * ka_ref[...])
    rkk = r * k2 * rk_ref[...]

    n = 2 * c
    row = lax.broadcasted_iota(jnp.int32, (n, n), 0)
    col = lax.broadcasted_iota(jnp.int32, (n, n), 1)
    strict = row > col
    incl = row >= col
    trow = lax.broadcasted_iota(jnp.int32, (c, c), 0)
    tcol = lax.broadcasted_iota(jnp.int32, (c, c), 1)
    tri = jnp.where(trow >= tcol, 1.0, 0.0).astype(_BF)
    lane = lax.broadcasted_iota(jnp.int32, (1, _LANES), 1)
    head0 = lane < _RWKV_HEAD
    hrow = lax.broadcasted_iota(jnp.int32, (_LANES, _LANES), 0)
    hcol = lax.broadcasted_iota(jnp.int32, (_LANES, _LANES), 1)
    same_head = (hrow < _RWKV_HEAD) == (hcol < _RWKV_HEAD)
    head_ones = jnp.where(same_head, 1.0, 0.0).astype(_BF)

    def stack(x):
        return jnp.concatenate([jnp.where(head0, x, 0.0), jnp.where(head0, 0.0, x)], axis=0)

    def twice(x):
        return jnp.concatenate([x, x], axis=0)

    def unstack(x):
        return jnp.where(head0, x[:c], x[c:])

    units = [(bi, p) for bi in range(nbat) for p in range(pairs)]
    ps = range(len(units))
    sls = [slice(p * _LANES, (p + 1) * _LANES) for _, p in units]

    def cut(x):
        return [x[bi * c:(bi + 1) * c, p * _LANES:(p + 1) * _LANES] for bi, p in units]

    lw = cut(logdec)
    cum = [_dot_exact_lhs(tri, x) for x in lw]
    kk0 = cut(kkr)
    ss = [_dot_hilo_rhs(x * x, head_ones) for x in kk0]
    kkp = [x / jnp.maximum(jnp.sqrt(q), 1e-12) for x, q in zip(kk0, ss)]
    ap = cut(a)
    k2p = cut(k2)
    rp = cut(r)
    vp = cut(v)
    gp = cut(g)
    vv = [twice(x) for x in vp]
    ginv = [jnp.exp(-x) for x in cum]
    at = [-kkp[p] * jnp.exp(cum[p] - lw[p]) for p in ps]
    bt = [kkp[p] * ap[p] * ginv[p] for p in ps]
    kt = [k2p[p] * ginv[p] for p in ps]
    rt = [rp[p] * jnp.exp(cum[p]) for p in ps]
    at2 = [stack(x) for x in at]
    rt2 = [stack(x) for x in rt]
    bt2 = [stack(x) for x in bt]
    kt2 = [stack(x) for x in kt]
    gq = [_bdot(jnp.concatenate([at2[p], rt2[p]], axis=0),
                jnp.concatenate([bt2[p], kt2[p]], axis=0), _NT) for p in ps]
    a_ab = [jnp.where(strict, gq[p][:n, :n], 0.0) for p in ps]
    a_ak = [jnp.where(strict, gq[p][:n, n:], 0.0) for p in ps]
    a_rb = [jnp.where(incl, gq[p][n:, :n], 0.0) for p in ps]
    a_rk = [jnp.where(incl, gq[p][n:, n:], 0.0) for p in ps]

    st = [st_ref[p] for p in ps]
    sx = [_bdot(jnp.concatenate([at[p], rt[p]], axis=0), st[p], _NT) for p in ps]
    av = [_bdot(jnp.concatenate([a_ak[p], a_rk[p]], axis=0), vv[p]) for p in ps]
    rhs = [twice(sx[p][:c]) + av[p][:n] for p in ps]
    u2 = _tri_solve(a_ab, rhs, n, _RWKV_BLK)
    uu = [unstack(x) for x in u2]
    y2 = [twice(sx[p][c:]) + _bdot(a_rb[p], twice(uu[p])) + av[p][n:] for p in ps]
    y = [unstack(x) for x in y2]

    cl = [x[c - 1:c, :] for x in cum]
    dl = [jnp.exp(cl[p] - cum[p]) for p in ps]
    upd = [_bdot(jnp.concatenate([uu[p], vp[p]], axis=0),
                 jnp.concatenate([kkp[p] * ap[p] * dl[p], k2p[p] * dl[p]], axis=0), _TN)
           for p in ps]
    for p in ps:
        st_ref[p] = jnp.where(same_head, st[p] * jnp.exp(cl[p]) + upd[p], 0.0)

    inv_n = 1.0 / _RWKV_HEAD
    mu = [_dot_hilo_rhs(x, head_ones) * inv_n for x in y]
    yc = [y[p] - mu[p] for p in ps]
    var = [_dot_hilo_rhs(x * x, head_ones) * inv_n for x in yc]
    bonus = [_dot_hilo_rhs(x, head_ones) for x in cut(rkk)]
    for p in ps:
        sl = sls[p]
        yn = yc[p] * lax.rsqrt(var[p] + _RWKV_LN_EPS) * lnw_ref[:, sl] + lnb_ref[:, sl]
        o_ref[units[p][0], :, sl] = (yn + bonus[p] * vp[p]) * gp[p]


_N_HGRN_IN, _N_RWKV_IN, _N_HGRN_SCRATCH = 6, 18, 6


def _mix0_kernel(*refs, hgrn_chunks, chunks_per_seq):
    a, b = _N_HGRN_IN, _N_HGRN_IN + _N_RWKV_IN
    hgrn_in, rwkv_in = refs[:a], refs[a:b]
    oa_ref, ob_ref = refs[b:b + 2]
    hgrn_sc = refs[b + 2:b + 2 + _N_HGRN_SCRATCH]
    rwkv_sc = refs[b + 2 + _N_HGRN_SCRATCH:]
    i = pl.program_id(0)

    @pl.when(i == 0)
    def _():
        hgrn_sc[0][...] = jnp.zeros_like(hgrn_sc[0])
        for ref in rwkv_sc:
            ref[...] = jnp.zeros_like(ref)

    ch = _HGRN_CHUNK
    for kk in range(hgrn_chunks):
        rows = pl.ds(kk * ch, ch)
        fresh = (i * hgrn_chunks + kk) % chunks_per_seq == 0
        _hgrn_body(*[r.at[rows] for r in hgrn_in[:4]], hgrn_in[4], hgrn_in[5], oa_ref.at[rows],
                   *hgrn_sc, fresh)
        if kk == 0:
            _rwkv_body(*rwkv_in, ob_ref, *rwkv_sc)


def _mix0(u, lb, onorm, mu, w0, w2, a0, a2, g2, k_k, k_a, r_k, ln_w, ln_b, batch, seq):
    t = u.shape[0]
    hw = lb.shape[0]
    width = w0.shape[0]
    c = _RWKV_CHUNK
    nc = seq // c
    ch = _HGRN_CHUNK
    assert (batch * seq) % (nc * ch) == 0 and seq % ch == 0
    hgrn_chunks = batch * seq // (nc * ch)
    col0 = 4 * hw
    tail = mu.shape[0] - 3 * width
    d_lora = w2.shape[0]
    a_lora = a2.shape[0]
    assert d_lora + a_lora == _LANES and tail - _LANES == g2.shape[0]
    assert col0 % width == 0 and (col0 + 3 * width) % tail == 0
    cb = col0 // width
    w2p = jnp.concatenate([w2, jnp.zeros((a_lora, width), _F32)], axis=0)
    a2p = jnp.concatenate([jnp.zeros((d_lora, width), _F32), a2], axis=0)
    u3 = u.reshape(batch, seq, u.shape[1])
    hrows = hgrn_chunks * ch

    def hcol(off):
        return pl.BlockSpec((hrows, hw), lambda i: (i, off))

    def colblk(j):
        return pl.BlockSpec((batch, c, width), lambda i: (0, i, cb + j))

    def vec(nn):
        return pl.BlockSpec((1, nn), lambda i: (0, 0))

    def mat(m):
        return pl.BlockSpec(m.shape, lambda i: (0, 0))

    row = lambda x: x.reshape(1, -1)
    tile = pltpu.VMEM((ch, hw), _F32)
    o_a, o_b = pl.pallas_call(
        functools.partial(_mix0_kernel, hgrn_chunks=hgrn_chunks, chunks_per_seq=seq // ch),
        grid=(nc,),
        in_specs=[hcol(0), hcol(1), hcol(2), hcol(3), vec(hw), vec(hw),
                  colblk(0), colblk(1), colblk(2),
                  pl.BlockSpec((batch, c, tail), lambda i: (0, i, (col0 + 3 * width) // tail)),
                  vec(width), vec(width), vec(width), vec(tail)]
        + [vec(width)] * 7 + [mat(w2p), mat(a2p), mat(g2)],
        out_specs=[pl.BlockSpec((hrows, hw), lambda i: (i, 0)),
                   pl.BlockSpec((batch, c, width), lambda i: (0, i, 0))],
        out_shape=[jax.ShapeDtypeStruct((t, hw), _F32),
                   jax.ShapeDtypeStruct((batch, seq, width), _F32)],
        scratch_shapes=[pltpu.VMEM((hw // _HGRN_HEAD, _HGRN_HEAD, _HGRN_HEAD), _F32)] + [tile] * 5
        + [pltpu.VMEM((batch * width // _LANES, _LANES, _LANES), _F32),
           pltpu.VMEM((batch, 1, width), _F32), pltpu.VMEM((batch, 1, width), _F32),
           pltpu.VMEM((batch, 1, width), _F32), pltpu.VMEM((batch, 1, tail), _F32)],
        compiler_params=_params("arbitrary"),
        name="mix0",
    )(u, u, u, u, row(lb), row(onorm), u3, u3, u3, u3,
      row(mu[:width]), row(mu[width:2 * width]), row(mu[2 * width:3 * width]), row(mu[3 * width:]),
      row(w0), row(a0), row(k_k), row(k_a), row(r_k), row(ln_w), row(ln_b), w2p, a2p, g2)
    return o_a, o_b.reshape(t, width)


def _post_kernel(*refs, n_mix):
    x_ref = refs[0]
    o_refs = refs[1:1 + n_mix]
    wo_refs = refs[1 + n_mix:1 + 2 * n_mix]
    (g_ref, wup_ref, wdn_ref, p_ref, wp_ref, pg_ref, wg_ref, out_ref,
     x1_s, h_s, acc_s) = refs[1 + 2 * n_mix:]
    kf = pl.program_id(1)

    @pl.when(kf == 0)
    def _():
        x1 = x_ref[...]
        for o_ref, wo_ref in zip(o_refs, wo_refs):
            x1 = x1 + _mm(o_ref[...].astype(_BF), wo_ref[...])
        x1_s[...] = x1
        h_s[...] = _rmsnorm(x1, g_ref[...]).astype(_BF)
        acc_s[...] = jnp.zeros_like(acc_s)

    act = jnp.square(jnp.maximum(_mm(h_s[...], wup_ref[...]), 0.0))
    acc_s[...] += _mm(act.astype(_BF), wdn_ref[...])

    @pl.when(kf == pl.num_programs(1) - 1)
    def _():
        x2 = x1_s[...] + acc_s[...]
        ple = _rmsnorm(_mm(p_ref[...].astype(_BF), wp_ref[...]), pg_ref[...])
        gate = _sigmoid(_mm(x2.astype(_BF), wg_ref[...]))
        out_ref[...] = x2 + ple * gate


def _post_mixer(x, mix, w_out_parts, g, w_up, w_down, p, w_p, p_g, w_g, tm=512, tf=1024):
    t, d = x.shape
    dff = w_up.shape[1]
    n_mix = len(mix)
    tok = lambda w: pl.BlockSpec((tm, w), lambda i, k: (i, 0))
    full = lambda m: pl.BlockSpec(m.shape, lambda i, k: (0, 0))
    in_specs = ([tok(d)] + [tok(o.shape[1]) for o in mix] + [full(w) for w in w_out_parts]
                + [pl.BlockSpec((1, d), lambda i, k: (0, 0)),
                   pl.BlockSpec((d, tf), lambda i, k: (0, k)),
                   pl.BlockSpec((tf, d), lambda i, k: (k, 0)),
                   tok(p.shape[1]), full(w_p),
                   pl.BlockSpec((1, d), lambda i, k: (0, 0)), full(w_g)])
    return pl.pallas_call(
        functools.partial(_post_kernel, n_mix=n_mix),
        grid=(t // tm, dff // tf),
        in_specs=in_specs,
        out_specs=pl.BlockSpec((tm, d), lambda i, k: (i, 0)),
        out_shape=jax.ShapeDtypeStruct((t, d), _F32),
        scratch_shapes=[pltpu.VMEM((tm, d), _F32), pltpu.VMEM((tm, d), _BF),
                        pltpu.VMEM((tm, d), _F32)],
        compiler_params=_params("parallel", "arbitrary"),
        name="post_mixer",
    )(x, *mix, *w_out_parts, g.reshape(1, d), w_up, w_down, p, w_p, p_g.reshape(1, d), w_g)


def _qk_prep_kernel(x_ref, g_ref, w_ref, cos_ref, sin_ref, qg_ref, kg_ref,
                    q_ref, k_ref, vt_ref, km_ref):
    d = q_ref.shape[1]
    heads = d // _LANES
    j = pl.program_id(1)
    cos = cos_ref[...]
    sin = sin_ref[...]
    qkv = _mm(_rmsnorm(x_ref[...], g_ref[...]).astype(_BF), w_ref[...])

    def rope(x, gain):
        xn = _rmsnorm(x, gain)
        return xn * cos + pltpu.roll(xn, _LANES // 2, axis=1) * sin

    for h in range(heads):
        sl = slice(h * _LANES, (h + 1) * _LANES)
        q_ref[:, sl] = rope(qkv[:, sl], qg_ref[...])
        kr = rope(qkv[:, d + h * _LANES:d + (h + 1) * _LANES], kg_ref[...])
        k_ref[:, sl] = kr.astype(_BF)
        km_ref[0, pl.ds(j, 1), :, sl] = jnp.mean(kr, axis=0, keepdims=True)[None]
    vt_ref[0, 0] = qkv[:, 2 * d:].T.astype(_BF)


def _qk_prep(x, g, w_bf, q_gain, k_gain, batch, seq):
    t, d = x.shape
    blk = _MOBA_BLOCK
    nb = seq // blk
    half = _LANES // 2
    inv_freq = jnp.power(_ROPE_THETA, -jnp.arange(half, dtype=_F32) / half)
    a_in = jnp.arange(blk, dtype=_F32)[None, :, None] * inv_freq
    a_blk = (jnp.arange(nb, dtype=_F32) * blk)[:, None, None] * inv_freq
    cos_h = (jnp.cos(a_blk) * jnp.cos(a_in) - jnp.sin(a_blk) * jnp.sin(a_in)).reshape(seq, half)
    sin_h = (jnp.sin(a_blk) * jnp.cos(a_in) + jnp.cos(a_blk) * jnp.sin(a_in)).reshape(seq, half)
    cos = jnp.concatenate([cos_h, cos_h], axis=1)
    sin = jnp.concatenate([-sin_h, sin_h], axis=1)
    tok = lambda w: pl.BlockSpec((blk, w), lambda b, j: (b * nb + j, 0))
    tab = pl.BlockSpec((blk, _LANES), lambda b, j: (j, 0))
    vec = pl.BlockSpec((1, _LANES), lambda b, j: (0, 0))
    return pl.pallas_call(
        _qk_prep_kernel,
        grid=(batch, nb),
        in_specs=[tok(d), pl.BlockSpec((1, d), lambda b, j: (0, 0)),
                  pl.BlockSpec(w_bf.shape, lambda b, j: (0, 0)), tab, tab, vec, vec],
        out_specs=[tok(d), tok(d), pl.BlockSpec((1, 1, d, blk), lambda b, j: (b, j, 0, 0)),
                   pl.BlockSpec((1, nb, 1, d), lambda b, j: (b, 0, 0, 0))],
        out_shape=[jax.ShapeDtypeStruct((t, d), _F32), jax.ShapeDtypeStruct((t, d), _BF),
                   jax.ShapeDtypeStruct((batch, nb, d, blk), _BF),
                   jax.ShapeDtypeStruct((batch, nb, 1, d), _F32)],
        compiler_params=_params("parallel", "arbitrary"),
        name="qk_prep",
    )(x, g.reshape(1, d), w_bf, cos, sin, q_gain.reshape(1, _LANES), k_gain.reshape(1, _LANES))


def _moba_kernel(q_ref, k_ref, vt_ref, km_ref, o_ref, sel_s, m_s, l_s, acc_s):
    blk = _MOBA_BLOCK
    heads = q_ref.shape[1] // _LANES
    nb = km_ref.shape[1]
    i = pl.program_id(2)
    neg = -jnp.inf
    hs = range(heads)
    sls = [slice(h * _LANES, (h + 1) * _LANES) for h in hs]

    bidx = lax.broadcasted_iota(jnp.int32, (nb, blk), 0)
    qs = [q_ref[:, sls[h]] for h in hs]
    qts = [(qs[h] * (_LANES ** -0.5)).T.astype(_BF) for h in hs]
    gate = [_dot3(km_ref[0, :, sls[h]], qs[h], _NT) for h in hs]
    gate = [jnp.where(bidx < i, gate[h], neg) for h in hs]
    sel = [jnp.zeros((nb, blk), _F32) for h in hs]
    for _ in range(_MOBA_TOPK):
        m = [jnp.max(gate[h], axis=0, keepdims=True) for h in hs]
        cand = [jnp.where((gate[h] == m[h]) & (m[h] > neg), bidx, nb) for h in hs]
        pick = [bidx == jnp.min(cand[h], axis=0, keepdims=True) for h in hs]
        sel = [jnp.where(pick[h], 1.0, sel[h]) for h in hs]
        gate = [jnp.where(pick[h], neg, gate[h]) for h in hs]
    for h in hs:
        sel_s[h] = sel[h]

    ki = lax.broadcasted_iota(jnp.int32, (blk, blk), 0)
    qi = lax.broadcasted_iota(jnp.int32, (blk, blk), 1)
    causal = ki <= qi
    start = pl.multiple_of(i * blk, blk)
    s = [_mm(k_ref[pl.ds(start, blk), sls[h]], qts[h]) for h in hs]
    s = [jnp.where(causal, s[h], neg) for h in hs]
    m = [jnp.max(s[h], axis=0, keepdims=True).astype(_BF).astype(_F32) for h in hs]
    ones = jnp.ones((16, blk), _BF)
    pr = [jnp.exp((s[h] - m[h]).astype(_BF)) for h in hs]
    pv = [_mm(jnp.concatenate([vt_ref[0, i, sls[h], :], ones], axis=0), pr[h]) for h in hs]
    for h in hs:
        m_s[h] = m[h]
        l_s[h] = pv[h][_LANES:_LANES + 1]
        acc_s[h] = pv[h][:_LANES]

    def past(j, carry):
        kstart = pl.multiple_of(j * blk, blk)
        s = [_mm(k_ref[pl.ds(kstart, blk), sls[h]], qts[h]).astype(_BF) for h in hs]
        s = [jnp.where(sel_s[h, pl.ds(j, 1), :] > 0.5, s[h], neg) for h in hs]
        m_old = [m_s[h] for h in hs]
        m_new = [jnp.maximum(m_old[h], jnp.max(s[h], axis=0, keepdims=True).astype(_F32))
                 for h in hs]
        alpha = [jnp.exp(m_old[h] - m_new[h]) for h in hs]
        pr = [jnp.exp(s[h] - m_new[h].astype(_BF)) for h in hs]
        pv = [_mm(jnp.concatenate([vt_ref[0, j, sls[h], :], ones], axis=0), pr[h]) for h in hs]
        for h in hs:
            m_s[h] = m_new[h]
            l_s[h] = alpha[h] * l_s[h] + pv[h][_LANES:_LANES + 1]
            acc_s[h] = alpha[h] * acc_s[h] + pv[h][:_LANES]
        return carry

    lax.fori_loop(0, i, past, 0)
    for h in hs:
        o_ref[:, sls[h]] = (acc_s[h] / l_s[h]).T


def _moba(q, k, vt, kmean, batch, seq):
    t, d = q.shape
    blk = _MOBA_BLOCK
    nb = seq // blk
    w = _MOBA_HEADS_PER_STEP * _LANES
    qspec = pl.BlockSpec((blk, w), lambda b, h, i: (b * nb + i, h))
    return pl.pallas_call(
        _moba_kernel,
        grid=(batch, d // w, nb),
        in_specs=[qspec,
                  pl.BlockSpec((seq, w), lambda b, h, i: (b, h)),
                  pl.BlockSpec((1, nb, w, blk), lambda b, h, i: (b, 0, h, 0)),
                  pl.BlockSpec((1, nb, w), lambda b, h, i: (b, 0, h))],
        out_specs=qspec,
        out_shape=jax.ShapeDtypeStruct((t, d), _F32),
        scratch_shapes=[pltpu.VMEM((_MOBA_HEADS_PER_STEP, nb, blk), _F32),
                        pltpu.VMEM((_MOBA_HEADS_PER_STEP, 1, blk), _F32),
                        pltpu.VMEM((_MOBA_HEADS_PER_STEP, 1, blk), _F32),
                        pltpu.VMEM((_MOBA_HEADS_PER_STEP, _LANES, blk), _F32)],
        compiler_params=_params("parallel", "parallel", "arbitrary"),
        name="moba_attention",
    )(q, k, vt, kmean)


def kernel(x, p, attn_norm, mlp_norm, w_in_ar, w_out_ar, hgrn_lb, hgrn_onorm, rwkv_mu, rwkv_w0, rwkv_w2, rwkv_a0, rwkv_a2, rwkv_g2, rwkv_kk, rwkv_ka, rwkv_rk, rwkv_ln_w, rwkv_ln_b, w_qkv, w_o_attn, q_norm, k_norm, w_up, w_down, ple_proj, ple_norm, ple_gate):
    batch, seq, d = x.shape
    depth = p.shape[0]
    t = batch * seq
    hw = hgrn_onorm.shape[1]
    bf = lambda w: w.astype(_BF)
    lb_all = jnp.cumsum(jax.nn.softmax(hgrn_lb.astype(_F32), axis=0), axis=0)
    xt = x.reshape(t, d)
    for l in range(depth):
        if l % 2 == 0:
            e = l // 2
            u = _norm_matmul(xt, attn_norm[l], bf(w_in_ar[e]))
            o_a, o_b = _mix0(u, lb_all[l], hgrn_onorm[e], rwkv_mu[e], rwkv_w0[e], rwkv_w2[e],
                             rwkv_a0[e], rwkv_a2[e], rwkv_g2[e], rwkv_kk[e], rwkv_ka[e],
                             rwkv_rk[e], rwkv_ln_w[e], rwkv_ln_b[e], batch, seq)
            w_o = bf(w_out_ar[e])
            mix, w_o = [o_a, o_b], [w_o[:hw], w_o[hw:]]
        else:
            o = l // 2
            q, k, vt, kmean = _qk_prep(xt, attn_norm[l], bf(w_qkv[o]), q_norm[o], k_norm[o],
                                       batch, seq)
            kmean = kmean.reshape(batch, -1, d)
            mix, w_o = [_moba(q, k, vt, kmean, batch, seq)], [bf(w_o_attn[o])]
        xt = _post_mixer(xt, mix, w_o, mlp_norm[l], bf(w_up[l]), bf(w_down[l]),
                         p[l].reshape(t, -1), bf(ple_proj[l]), ple_norm[l], bf(ple_gate[l]))
    return xt.reshape(batch, seq, d)
```

```python
import functools
import math

import jax
import jax.numpy as jnp
from jax import lax
from jax.experimental import pallas as pl
from jax.experimental.pallas import tpu as pltpu

_F32 = jnp.float32
_BF = jnp.bfloat16

_NORM_EPS = 1e-6
_RWKV_LN_EPS = 64e-5
_ROPE_THETA = 10000.0

_HGRN_HEAD = 128
_HGRN_CHUNK = 128
_HGRN_SUB = 16
_RWKV_HEAD = 64
_RWKV_CHUNK = 64
_RWKV_BLK = 16
_MOBA_BLOCK = 256
_MOBA_TOPK = 3
_MOBA_HEADS_PER_STEP = 8
_LANES = 128
_VMEM_LIMIT = 56 * 1024 * 1024

_NN = (((1,), (0,)), ((), ()))
_NT = (((1,), (1,)), ((), ()))
_TN = (((0,), (0,)), ((), ()))


def _mm(a, b, dims=_NN):
    return lax.dot_general(a, b, dims, preferred_element_type=_F32)


def _bdot(a, b, dims=_NN):
    return _mm(a.astype(_BF), b.astype(_BF), dims)


def _split2(a):
    hi = a.astype(_BF)
    lo = (a - hi.astype(_F32)).astype(_BF)
    return hi, lo


def _split3(a):
    hi = a.astype(_BF)
    r = a - hi.astype(_F32)
    mid = r.astype(_BF)
    lo = (r - mid.astype(_F32)).astype(_BF)
    return hi, mid, lo


def _dot3(a, b, dims=_NN):
    ah, al = _split2(a)
    bh, bl = _split2(b)
    return _mm(ah, bh, dims) + (_mm(ah, bl, dims) + _mm(al, bh, dims))


def _dot3_shared(lhs, b):
    bh, bl = _split2(b)
    parts = [_split2(a) for a in lhs]
    his = [h for h, _ in parts]
    tot = sum(a.shape[0] for a in lhs)
    r1 = _mm(jnp.concatenate(his + [l for _, l in parts], axis=0), bh)
    r2 = _mm(jnp.concatenate(his, axis=0), bl)
    out, off = [], 0
    for a in lhs:
        m = a.shape[0]
        out.append(r1[off:off + m] + (r1[tot + off:tot + off + m] + r2[off:off + m]))
        off += m
    return out


def _dot_exact_lhs(a_bf, b):
    w = b.shape[1]
    r = _mm(a_bf, jnp.concatenate(_split3(b), axis=1))
    return r[:, :w] + (r[:, w:2 * w] + r[:, 2 * w:])


def _dot_hilo_rhs(a, b_bf):
    n = a.shape[0]
    r = _mm(jnp.concatenate(_split2(a), axis=0), b_bf)
    return r[:n] + r[n:]


def _rmsnorm(x, g):
    return x * lax.rsqrt(jnp.mean(x * x, axis=-1, keepdims=True) + _NORM_EPS) * g


def _sigmoid(x):
    return 1.0 / (1.0 + jnp.exp(-x))


def _silu(x):
    return x * _sigmoid(x)


def _params(*sem):
    return pltpu.CompilerParams(dimension_semantics=sem, vmem_limit_bytes=_VMEM_LIMIT)


def _norm_matmul_kernel(x_ref, g_ref, w_ref, o_ref):
    h = _rmsnorm(x_ref[...], g_ref[...]).astype(_BF)
    o_ref[...] = _mm(h, w_ref[...])


def _norm_matmul(x, g, w_bf, tm=512):
    t, d = x.shape
    n = w_bf.shape[1]
    return pl.pallas_call(
        _norm_matmul_kernel,
        grid=(t // tm,),
        in_specs=[
            pl.BlockSpec((tm, d), lambda i: (i, 0)),
            pl.BlockSpec((1, d), lambda i: (0, 0)),
            pl.BlockSpec((d, n), lambda i: (0, 0)),
        ],
        out_specs=pl.BlockSpec((tm, n), lambda i: (i, 0)),
        out_shape=jax.ShapeDtypeStruct((t, n), _F32),
        compiler_params=_params("parallel"),
        name="norm_matmul",
    )(x, g.reshape(1, d), w_bf)


def _hgrn_body(hq_ref, hf_ref, hi_ref, hg_ref, lb_ref, on_ref, o_ref,
               st_ref, q_s, k_s, b_s, c_s, o_s, fresh):
    ch, sub = _HGRN_CHUNK, _HGRN_SUB
    width = hq_ref.shape[1]
    hs = range(width // _HGRN_HEAD)
    sls = [slice(h * _HGRN_HEAD, (h + 1) * _HGRN_HEAD) for h in hs]

    lb = lb_ref[...]
    f = lb + (1.0 - lb) * _sigmoid(hf_ref[...])
    g = jnp.log(f)
    row = lax.broadcasted_iota(jnp.int32, (ch, ch), 0)
    col = lax.broadcasted_iota(jnp.int32, (ch, ch), 1)
    tri = jnp.where(row >= col, 1.0, 0.0).astype(_BF)
    b = _dot_exact_lhs(tri, g)
    q = _silu(hq_ref[...])
    k = 1.0 - f
    q_s[...] = q
    k_s[...] = k
    b_s[...] = b
    c_s[...] = b - jnp.log(jnp.maximum(k, 0.0))

    st = [jnp.where(fresh, 0.0, st_ref[h]) for h in hs]
    qe = q * jnp.exp(b)
    inter = [_bdot(qe[:, sls[h]], st[h], _NT) for h in hs]
    for h in hs:
        o_s[:, sls[h]] = inter[h]

    ones = jnp.ones((_HGRN_HEAD, _LANES), _BF)
    half = sub // 2
    rhalf = lax.broadcasted_iota(jnp.int32, (half, width), 0)
    for i in range(ch // sub):
        r0 = i * sub
        rows = slice(r0, r0 + sub)
        b_i = b_s[rows, :]
        q_i = q_s[rows, :]
        o_i = [o_s[rows, sls[h]] for h in hs]
        if i > 0:
            base = b_s[r0 - 1:r0, :]
            qh = q_i * jnp.exp(b_i - base)
            kh = k_s[0:r0, :] * jnp.exp(base - b_s[0:r0, :])
            sc = [_bdot(qh[:, sls[h]], kh[:, sls[h]], _NT) for h in hs]
            off = [_bdot(sc[h], hi_ref[0:r0, sls[h]]) for h in hs]
            o_i = [o_i[h] + off[h] for h in hs]
        b_lo, b_hi = b_i[:half], b_i[half:]
        q_lo, q_hi = q_i[:half], q_i[half:]
        parts = []
        for s in range(sub):
            crow = c_s[r0 + s:r0 + s + 1, :]
            if s < half:
                p_lo = q_lo * jnp.exp(jnp.minimum(b_lo - crow, 0.0))
                parts.append(jnp.where(rhalf >= s, p_lo, 0.0))
                parts.append(q_hi * jnp.exp(b_hi - crow))
            else:
                p_hi = q_hi * jnp.exp(jnp.minimum(b_hi - crow, 0.0))
                parts.append(jnp.where(rhalf >= s - half, p_hi, 0.0))
        pst = jnp.concatenate(parts, axis=0).astype(_BF)
        sb = [_mm(pst[:, sls[h]], ones) for h in hs]
        o_lo = [o_i[h][:half] for h in hs]
        o_hi = [o_i[h][half:] for h in hs]
        for s in range(sub):
            vrow = hi_ref[r0 + s:r0 + s + 1, :]
            if s < half:
                at = s * sub
                o_lo = [o_lo[h] + sb[h][at:at + half] * vrow[:, sls[h]] for h in hs]
                at += half
            else:
                at = half * sub + (s - half) * half
            o_hi = [o_hi[h] + sb[h][at:at + half] * vrow[:, sls[h]] for h in hs]
        for h in hs:
            o_s[r0:r0 + half, sls[h]] = o_lo[h]
            o_s[r0 + half:r0 + sub, sls[h]] = o_hi[h]

    bl = b_s[ch - 1:ch, :]
    kd = k * jnp.exp(bl - b)
    ebl = jnp.exp(bl)
    upd = [_bdot(hi_ref[:, sls[h]], kd[:, sls[h]], _TN) for h in hs]
    for h in hs:
        st_ref[h] = st[h] * ebl[:, sls[h]] + upd[h]

    gate = _silu(hg_ref[...])
    for h in hs:
        o_ref[:, sls[h]] = _rmsnorm(o_s[:, sls[h]], on_ref[:, sls[h]]) * gate[:, sls[h]]


def _tri_solve(a_list, rhs_list, n, blk):
    row = lax.broadcasted_iota(jnp.int32, (n, n), 0)
    col = lax.broadcasted_iota(jnp.int32, (n, n), 1)
    eye = jnp.where(row == col, 1.0, 0.0)
    shift = int(math.log2(blk))
    same = (row >> shift) == (col >> shift)
    ad = [jnp.where(same, a, 0.0) for a in a_list]
    low = [a - d for a, d in zip(a_list, ad)]
    ks = range(len(a_list))
    x = [eye + d for d in ad]
    p = [_dot3_shared([d], d)[0] for d in ad]
    for _ in range(shift - 2):
        both = [_dot3_shared([x[k], p[k]], p[k]) for k in ks]
        x = [x[k] + both[k][0] for k in ks]
        p = [both[k][1] for k in ks]
    x = [x[k] + _dot3_shared([x[k]], p[k])[0] for k in ks]
    both = [_bdot(x[k], jnp.concatenate([low[k], rhs_list[k]], axis=1)) for k in ks]
    nmat = [both[k][:, :n].astype(_BF) for k in ks]
    term = [both[k][:, n:] for k in ks]
    u = term
    for _ in range((n // 2) // blk - 1):
        term = [_mm(nmat[k], term[k].astype(_BF)) for k in ks]
        u = [u[k] + term[k] for k in ks]
    return u


def _rwkv_body(r_ref, k_ref, v_ref, t_ref, mur_ref, muk_ref, muv_ref, mut_ref,
               w0_ref, a0_ref, kk_ref, ka_ref, rk_ref, lnw_ref, lnb_ref,
               w2_ref, a2_ref, g2_ref, o_ref,
               st_ref, pr_ref, pk_ref, pv_ref, pt_ref):
    c = _RWKV_CHUNK
    nbat = r_ref.shape[0]
    pairs = r_ref.shape[2] // _LANES

    def shift_mix(x_ref, prev_ref, mu_ref):
        outs = []
        for bi in range(nbat):
            x = x_ref[bi]
            first = lax.broadcasted_iota(jnp.int32, x.shape, 0) == 0
            xs = jnp.where(first, prev_ref[bi], pltpu.roll(x, 1, axis=0))
            prev_ref[bi] = x_ref[bi, c - 1:c, :]
            outs.append(x + (xs - x) * mu_ref[...])
        return jnp.concatenate(outs, axis=0)

    r = shift_mix(r_ref, pr_ref, mur_ref)
    k = shift_mix(k_ref, pk_ref, muk_ref)
    v = shift_mix(v_ref, pv_ref, muv_ref)
    tl = shift_mix(t_ref, pt_ref, mut_ref)
    lo = tl[:, :_LANES]
    gd = tl[:, _LANES:]

    wpre = w0_ref[...] + _dot3(jnp.tanh(lo), w2_ref[...])
    sp = jnp.maximum(-wpre, 0.0) + jnp.log(1.0 + jnp.exp(-jnp.abs(wpre)))
    logdec = -jnp.exp(-sp - 0.5)
    a = _sigmoid(a0_ref[...] + _dot3(lo, a2_ref[...]))
    g = _dot3(_sigmoid(gd), g2_ref[...])
    kkr = k * kk_ref[...]
    k2 = k * (1.0 + (a - 1.0) * ka_ref[...])
    rkk = r * k2 * rk_ref[...]

    n = 2 * c
    row = lax.broadcasted_iota(jnp.int32, (n, n), 0)
    col = lax.broadcasted_iota(jnp.int32, (n, n), 1)
    strict = row > col
    incl = row >= col
    trow = lax.broadcasted_iota(jnp.int32, (c, c), 0)
    tcol = lax.broadcasted_iota(jnp.int32, (c, c), 1)
    tri = jnp.where(trow >= tcol, 1.0, 0.0).astype(_BF)
    lane = lax.broadcasted_iota(jnp.int32, (1, _LANES), 1)
    head0 = lane < _RWKV_HEAD
    hrow = lax.broadcasted_iota(jnp.int32, (_LANES, _LANES), 0)
    hcol = lax.broadcasted_iota(jnp.int32, (_LANES, _LANES), 1)
    same_head = (hrow < _RWKV_HEAD) == (hcol < _RWKV_HEAD)
    head_ones = jnp.where(same_head, 1.0, 0.0).astype(_BF)

    def stack(x):
        return jnp.concatenate([jnp.where(head0, x, 0.0), jnp.where(head0, 0.0, x)], axis=0)

    def twice(x):
        return jnp.concatenate([x, x], axis=0)

    def unstack(x):
        return jnp.where(head0, x[:c], x[c:])

    units = [(bi, p) for bi in range(nbat) for p in range(pairs)]
    ps = range(len(units))
    sls = [slice(p * _LANES, (p + 1) * _LANES) for _, p in units]

    def cut(x):
        return [x[bi * c:(bi + 1) * c, p * _LANES:(p + 1) * _LANES] for bi, p in units]

    lw = cut(logdec)
    cum = [_dot_exact_lhs(tri, x) for x in lw]
    kk0 = cut(kkr)
    ss = [_dot_hilo_rhs(x * x, head_ones) for x in kk0]
    kkp = [x / jnp.maximum(jnp.sqrt(q), 1e-12) for x, q in zip(kk0, ss)]
    ap = cut(a)
    k2p = cut(k2)
    rp = cut(r)
    vp = cut(v)
    gp = cut(g)
    vv = [twice(x) for x in vp]
    ginv = [jnp.exp(-x) for x in cum]
    at = [-kkp[p] * jnp.exp(cum[p] - lw[p]) for p in ps]
    bt = [kkp[p] * ap[p] * ginv[p] for p in ps]
    kt = [k2p[p] * ginv[p] for p in ps]
    rt = [rp[p] * jnp.exp(cum[p]) for p in ps]
    at2 = [stack(x) for x in at]
    rt2 = [stack(x) for x in rt]
    bt2 = [stack(x) for x in bt]
    kt2 = [stack(x) for x in kt]
    gq = [_bdot(jnp.concatenate([at2[p], rt2[p]], axis=0),
                jnp.concatenate([bt2[p], kt2[p]], axis=0), _NT) for p in ps]
    a_ab = [jnp.where(strict, gq[p][:n, :n], 0.0) for p in ps]
    a_ak = [jnp.where(strict, gq[p][:n, n:], 0.0) for p in ps]
    a_rb = [jnp.where(incl, gq[p][n:, :n], 0.0) for p in ps]
    a_rk = [jnp.where(incl, gq[p][n:, n:], 0.0) for p in ps]

    st = [st_ref[p] for p in ps]
    sx = [_bdot(jnp.concatenate([at[p], rt[p]], axis=0), st[p], _NT) for p in ps]
    av = [_bdot(jnp.concatenate([a_ak[p], a_rk[p]], axis=0), vv[p]) for p in ps]
    rhs = [twice(sx[p][:c]) + av[p][:n] for p in ps]
    u2 = _tri_solve(a_ab, rhs, n, _RWKV_BLK)
    uu = [unstack(x) for x in u2]
    y2 = [twice(sx[p][c:]) + _bdot(a_rb[p], twice(uu[p])) + av[p][n:] for p in ps]
    y = [unstack(x) for x in y2]

    cl = [x[c - 1:c, :] for x in cum]
    dl = [jnp.exp(cl[p] - cum[p]) for p in ps]
    upd = [_bdot(jnp.concatenate([uu[p], vp[p]], axis=0),
                 jnp.concatenate([kkp[p] * ap[p] * dl[p], k2p[p] * dl[p]], axis=0), _TN)
           for p in ps]
    for p in ps:
        st_ref[p] = jnp.where(same_head, st[p] * jnp.exp(cl[p]) + upd[p], 0.0)

    inv_n = 1.0 / _RWKV_HEAD
    mu = [_dot_hilo_rhs(x, head_ones) * inv_n for x in y]
    yc = [y[p] - mu[p] for p in ps]
    var = [_dot_hilo_rhs(x * x, head_ones) * inv_n for x in yc]
    bonus = [_dot_hilo_rhs(x, head_ones) for x in cut(rkk)]
    for p in ps:
        sl = sls[p]
        yn = yc[p] * lax.rsqrt(var[p] + _RWKV_LN_EPS) * lnw_ref[:, sl] + lnb_ref[:, sl]
        o_ref[units[p][0], :, sl] = (yn + bonus[p] * vp[p]) * gp[p]


_N_HGRN_IN, _N_RWKV_IN, _N_HGRN_SCRATCH = 6, 18, 6


def _mix0_kernel(*refs, hgrn_chunks, chunks_per_seq):
    a, b = _N_HGRN_IN, _N_HGRN_IN + _N_RWKV_IN
    hgrn_in, rwkv_in = refs[:a], refs[a:b]
    oa_ref, ob_ref = refs[b:b + 2]
    hgrn_sc = refs[b + 2:b + 2 + _N_HGRN_SCRATCH]
    rwkv_sc = refs[b + 2 + _N_HGRN_SCRATCH:]
    i = pl.program_id(0)

    @pl.when(i == 0)
    def _():
        hgrn_sc[0][...] = jnp.zeros_like(hgrn_sc[0])
        for ref in rwkv_sc:
            ref[...] = jnp.zeros_like(ref)

    ch = _HGRN_CHUNK
    for kk in range(hgrn_chunks):
        rows = pl.ds(kk * ch, ch)
        fresh = (i * hgrn_chunks + kk) % chunks_per_seq == 0
        _hgrn_body(*[r.at[rows] for r in hgrn_in[:4]], hgrn_in[4], hgrn_in[5], oa_ref.at[rows],
                   *hgrn_sc, fresh)
        if kk == 0:
            _rwkv_body(*rwkv_in, ob_ref, *rwkv_sc)


def _mix0(u, lb, onorm, mu, w0, w2, a0, a2, g2, k_k, k_a, r_k, ln_w, ln_b, batch, seq):
    t = u.shape[0]
    hw = lb.shape[0]
    width = w0.shape[0]
    c = _RWKV_CHUNK
    nc = seq // c
    ch = _HGRN_CHUNK
    assert (batch * seq) % (nc * ch) == 0 and seq % ch == 0
    hgrn_chunks = batch * seq // (nc * ch)
    col0 = 4 * hw
    tail = mu.shape[0] - 3 * width
    d_lora = w2.shape[0]
    a_lora = a2.shape[0]
    assert d_lora + a_lora == _LANES and tail - _LANES == g2.shape[0]
    assert col0 % width == 0 and (col0 + 3 * width) % tail == 0
    cb = col0 // width
    w2p = jnp.concatenate([w2, jnp.zeros((a_lora, width), _F32)], axis=0)
    a2p = jnp.concatenate([jnp.zeros((d_lora, width), _F32), a2], axis=0)
    u3 = u.reshape(batch, seq, u.shape[1])
    hrows = hgrn_chunks * ch

    def hcol(off):
        return pl.BlockSpec((hrows, hw), lambda i: (i, off))

    def colblk(j):
        return pl.BlockSpec((batch, c, width), lambda i: (0, i, cb + j))

    def vec(nn):
        return pl.BlockSpec((1, nn), lambda i: (0, 0))

    def mat(m):
        return pl.BlockSpec(m.shape, lambda i: (0, 0))

    row = lambda x: x.reshape(1, -1)
    tile = pltpu.VMEM((ch, hw), _F32)
    o_a, o_b = pl.pallas_call(
        functools.partial(_mix0_kernel, hgrn_chunks=hgrn_chunks, chunks_per_seq=seq // ch),
        grid=(nc,),
        in_specs=[hcol(0), hcol(1), hcol(2), hcol(3), vec(hw), vec(hw),
                  colblk(0), colblk(1), colblk(2),
                  pl.BlockSpec((batch, c, tail), lambda i: (0, i, (col0 + 3 * width) // tail)),
                  vec(width), vec(width), vec(width), vec(tail)]
        + [vec(width)] * 7 + [mat(w2p), mat(a2p), mat(g2)],
        out_specs=[pl.BlockSpec((hrows, hw), lambda i: (i, 0)),
                   pl.BlockSpec((batch, c, width), lambda i: (0, i, 0))],
        out_shape=[jax.ShapeDtypeStruct((t, hw), _F32),
                   jax.ShapeDtypeStruct((batch, seq, width), _F32)],
        scratch_shapes=[pltpu.VMEM((hw // _HGRN_HEAD, _HGRN_HEAD, _HGRN_HEAD), _F32)] + [tile] * 5
        + [pltpu.VMEM((batch * width // _LANES, _LANES, _LANES), _F32),
           pltpu.VMEM((batch, 1, width), _F32), pltpu.VMEM((batch, 1, width), _F32),
           pltpu.VMEM((batch, 1, width), _F32), pltpu.VMEM((batch, 1, tail), _F32)],
        compiler_params=_params("arbitrary"),
        name="mix0",
    )(u, u, u, u, row(lb), row(onorm), u3, u3, u3, u3,
      row(mu[:width]), row(mu[width:2 * width]), row(mu[2 * width:3 * width]), row(mu[3 * width:]),
      row(w0), row(a0), row(k_k), row(k_a), row(r_k), row(ln_w), row(ln_b), w2p, a2p, g2)
    return o_a, o_b.reshape(t, width)


def _post_kernel(*refs, n_mix):
    x_ref = refs[0]
    o_refs = refs[1:1 + n_mix]
    wo_refs = refs[1 + n_mix:1 + 2 * n_mix]
    (g_ref, wup_ref, wdn_ref, p_ref, wp_ref, pg_ref, wg_ref, out_ref,
     x1_s, h_s, acc_s) = refs[1 + 2 * n_mix:]
    kf = pl.program_id(1)

    @pl.when(kf == 0)
    def _():
        x1 = x_ref[...]
        for o_ref, wo_ref in zip(o_refs, wo_refs):
            x1 = x1 + _mm(o_ref[...].astype(_BF), wo_ref[...])
        x1_s[...] = x1
        h_s[...] = _rmsnorm(x1, g_ref[...]).astype(_BF)
        acc_s[...] = jnp.zeros_like(acc_s)

    act = jnp.square(jnp.maximum(_mm(h_s[...], wup_ref[...]), 0.0))
    acc_s[...] += _mm(act.astype(_BF), wdn_ref[...])

    @pl.when(kf == pl.num_programs(1) - 1)
    def _():
        x2 = x1_s[...] + acc_s[...]
        ple = _rmsnorm(_mm(p_ref[...].astype(_BF), wp_ref[...]), pg_ref[...])
        gate = _sigmoid(_mm(x2.astype(_BF), wg_ref[...]))
        out_ref[...] = x2 + ple * gate


def _post_mixer(x, mix, w_out_parts, g, w_up, w_down, p, w_p, p_g, w_g, tm=512, tf=2048):
    t, d = x.shape
    dff = w_up.shape[1]
    n_mix = len(mix)
    tok = lambda w: pl.BlockSpec((tm, w), lambda i, k: (i, 0))
    full = lambda m: pl.BlockSpec(m.shape, lambda i, k: (0, 0))
    in_specs = ([tok(d)] + [tok(o.shape[1]) for o in mix] + [full(w) for w in w_out_parts]
                + [pl.BlockSpec((1, d), lambda i, k: (0, 0)),
                   pl.BlockSpec((d, tf), lambda i, k: (0, k)),
                   pl.BlockSpec((tf, d), lambda i, k: (k, 0)),
                   tok(p.shape[1]), full(w_p),
                   pl.BlockSpec((1, d), lambda i, k: (0, 0)), full(w_g)])
    return pl.pallas_call(
        functools.partial(_post_kernel, n_mix=n_mix),
        grid=(t // tm, dff // tf),
        in_specs=in_specs,
        out_specs=pl.BlockSpec((tm, d), lambda i, k: (i, 0)),
        out_shape=jax.ShapeDtypeStruct((t, d), _F32),
        scratch_shapes=[pltpu.VMEM((tm, d), _F32), pltpu.VMEM((tm, d), _BF),
                        pltpu.VMEM((tm, d), _F32)],
        compiler_params=_params("parallel", "arbitrary"),
        name="post_mixer",
    )(x, *mix, *w_out_parts, g.reshape(1, d), w_up, w_down, p, w_p, p_g.reshape(1, d), w_g)


def _qk_prep_kernel(x_ref, g_ref, w_ref, cos_ref, sin_ref, qg_ref, kg_ref,
                    q_ref, k_ref, vt_ref, km_ref):
    d = q_ref.shape[1]
    heads = d // _LANES
    j = pl.program_id(1)
    cos = cos_ref[...]
    sin = sin_ref[...]
    qkv = _mm(_rmsnorm(x_ref[...], g_ref[...]).astype(_BF), w_ref[...])

    def rope(x, gain):
        xn = _rmsnorm(x, gain)
        return xn * cos + pltpu.roll(xn, _LANES // 2, axis=1) * sin

    for h in range(heads):
        sl = slice(h * _LANES, (h + 1) * _LANES)
        q_ref[:, sl] = rope(qkv[:, sl], qg_ref[...])
        kr = rope(qkv[:, d + h * _LANES:d + (h + 1) * _LANES], kg_ref[...])
        k_ref[:, sl] = kr.astype(_BF)
        km_ref[0, pl.ds(j, 1), :, sl] = jnp.mean(kr, axis=0, keepdims=True)[None]
    vt_ref[0, 0] = qkv[:, 2 * d:].T.astype(_BF)


def _qk_prep(x, g, w_bf, q_gain, k_gain, batch, seq):
    t, d = x.shape
    blk = _MOBA_BLOCK
    nb = seq // blk
    half = _LANES // 2
    inv_freq = jnp.power(_ROPE_THETA, -jnp.arange(half, dtype=_F32) / half)
    a_in = jnp.arange(blk, dtype=_F32)[None, :, None] * inv_freq
    a_blk = (jnp.arange(nb, dtype=_F32) * blk)[:, None, None] * inv_freq
    cos_h = (jnp.cos(a_blk) * jnp.cos(a_in) - jnp.sin(a_blk) * jnp.sin(a_in)).reshape(seq, half)
    sin_h = (jnp.sin(a_blk) * jnp.cos(a_in) + jnp.cos(a_blk) * jnp.sin(a_in)).reshape(seq, half)
    cos = jnp.concatenate([cos_h, cos_h], axis=1)
    sin = jnp.concatenate([-sin_h, sin_h], axis=1)
    tok = lambda w: pl.BlockSpec((blk, w), lambda b, j: (b * nb + j, 0))
    tab = pl.BlockSpec((blk, _LANES), lambda b, j: (j, 0))
    vec = pl.BlockSpec((1, _LANES), lambda b, j: (0, 0))
    return pl.pallas_call(
        _qk_prep_kernel,
        grid=(batch, nb),
        in_specs=[tok(d), pl.BlockSpec((1, d), lambda b, j: (0, 0)),
                  pl.BlockSpec(w_bf.shape, lambda b, j: (0, 0)), tab, tab, vec, vec],
        out_specs=[tok(d), tok(d), pl.BlockSpec((1, 1, d, blk), lambda b, j: (b, j, 0, 0)),
                   pl.BlockSpec((1, nb, 1, d), lambda b, j: (b, 0, 0, 0))],
        out_shape=[jax.ShapeDtypeStruct((t, d), _F32), jax.ShapeDtypeStruct((t, d), _BF),
                   jax.ShapeDtypeStruct((batch, nb, d, blk), _BF),
                   jax.ShapeDtypeStruct((batch, nb, 1, d), _F32)],
        compiler_params=_params("parallel", "arbitrary"),
        name="qk_prep",
    )(x, g.reshape(1, d), w_bf, cos, sin, q_gain.reshape(1, _LANES), k_gain.reshape(1, _LANES))


def _moba_kernel(q_ref, k_ref, vt_ref, km_ref, o_ref, sel_s, m_s, l_s, acc_s):
    blk = _MOBA_BLOCK
    heads = q_ref.shape[1] // _LANES
    nb = km_ref.shape[1]
    i = pl.program_id(2)
    neg = -jnp.inf
    hs = range(heads)
    sls = [slice(h * _LANES, (h + 1) * _LANES) for h in hs]

    bidx = lax.broadcasted_iota(jnp.int32, (nb, blk), 0)
    qs = [q_ref[:, sls[h]] for h in hs]
    qts = [(qs[h] * (_LANES ** -0.5)).T.astype(_BF) for h in hs]
    gate = [_dot3(km_ref[0, :, sls[h]], qs[h], _NT) for h in hs]
    gate = [jnp.where(bidx < i, gate[h], neg) for h in hs]
    sel = [jnp.zeros((nb, blk), _F32) for h in hs]
    for _ in range(_MOBA_TOPK):
        m = [jnp.max(gate[h], axis=0, keepdims=True) for h in hs]
        cand = [jnp.where((gate[h] == m[h]) & (m[h] > neg), bidx, nb) for h in hs]
        pick = [bidx == jnp.min(cand[h], axis=0, keepdims=True) for h in hs]
        sel = [jnp.where(pick[h], 1.0, sel[h]) for h in hs]
        gate = [jnp.where(pick[h], neg, gate[h]) for h in hs]
    for h in hs:
        sel_s[h] = sel[h]

    ki = lax.broadcasted_iota(jnp.int32, (blk, blk), 0)
    qi = lax.broadcasted_iota(jnp.int32, (blk, blk), 1)
    causal = ki <= qi
    start = pl.multiple_of(i * blk, blk)
    s = [_mm(k_ref[pl.ds(start, blk), sls[h]], qts[h]) for h in hs]
    s = [jnp.where(causal, s[h], neg) for h in hs]
    m = [jnp.max(s[h], axis=0, keepdims=True).astype(_BF).astype(_F32) for h in hs]
    ones = jnp.ones((16, blk), _BF)
    pr = [jnp.exp((s[h] - m[h]).astype(_BF)) for h in hs]
    pv = [_mm(jnp.concatenate([vt_ref[0, i, sls[h], :], ones], axis=0), pr[h]) for h in hs]
    for h in hs:
        m_s[h] = m[h]
        l_s[h] = pv[h][_LANES:_LANES + 1]
        acc_s[h] = pv[h][:_LANES]

    def past(j, carry):
        kstart = pl.multiple_of(j * blk, blk)
        s = [_mm(k_ref[pl.ds(kstart, blk), sls[h]], qts[h]).astype(_BF) for h in hs]
        s = [jnp.where(sel_s[h, pl.ds(j, 1), :] > 0.5, s[h], neg) for h in hs]
        m_old = [m_s[h] for h in hs]
        m_new = [jnp.maximum(m_old[h], jnp.max(s[h], axis=0, keepdims=True).astype(_F32))
                 for h in hs]
        alpha = [jnp.exp(m_old[h] - m_new[h]) for h in hs]
        pr = [jnp.exp(s[h] - m_new[h].astype(_BF)) for h in hs]
        pv = [_mm(jnp.concatenate([vt_ref[0, j, sls[h], :], ones], axis=0), pr[h]) for h in hs]
        for h in hs:
            m_s[h] = m_new[h]
            l_s[h] = alpha[h] * l_s[h] + pv[h][_LANES:_LANES + 1]
            acc_s[h] = alpha[h] * acc_s[h] + pv[h][:_LANES]
        return carry

    lax.fori_loop(0, i, past, 0)
    for h in hs:
        o_ref[:, sls[h]] = (acc_s[h] / l_s[h]).T


def _moba(q, k, vt, kmean, batch, seq):
    t, d = q.shape
    blk = _MOBA_BLOCK
    nb = seq // blk
    w = _MOBA_HEADS_PER_STEP * _LANES
    qspec = pl.BlockSpec((blk, w), lambda b, h, i: (b * nb + i, h))
    return pl.pallas_call(
        _moba_kernel,
        grid=(batch, d // w, nb),
        in_specs=[qspec,
                  pl.BlockSpec((seq, w), lambda b, h, i: (b, h)),
                  pl.BlockSpec((1, nb, w, blk), lambda b, h, i: (b, 0, h, 0)),
                  pl.BlockSpec((1, nb, w), lambda b, h, i: (b, 0, h))],
        out_specs=qspec,
        out_shape=jax.ShapeDtypeStruct((t, d), _F32),
        scratch_shapes=[pltpu.VMEM((_MOBA_HEADS_PER_STEP, nb, blk), _F32),
                        pltpu.VMEM((_MOBA_HEADS_PER_STEP, 1, blk), _F32),
                        pltpu.VMEM((_MOBA_HEADS_PER_STEP, 1, blk), _F32),
                        pltpu.VMEM((_MOBA_HEADS_PER_STEP, _LANES, blk), _F32)],
        compiler_params=_params("parallel", "parallel", "arbitrary"),
        name="moba_attention",
    )(q, k, vt, kmean)


def kernel(x, p, attn_norm, mlp_norm, w_in_ar, w_out_ar, hgrn_lb, hgrn_onorm, rwkv_mu, rwkv_w0, rwkv_w2, rwkv_a0, rwkv_a2, rwkv_g2, rwkv_kk, rwkv_ka, rwkv_rk, rwkv_ln_w, rwkv_ln_b, w_qkv, w_o_attn, q_norm, k_norm, w_up, w_down, ple_proj, ple_norm, ple_gate):
    batch, seq, d = x.shape
    depth = p.shape[0]
    t = batch * seq
    hw = hgrn_onorm.shape[1]
    bf = lambda w: w.astype(_BF)
    lb_all = jnp.cumsum(jax.nn.softmax(hgrn_lb.astype(_F32), axis=0), axis=0)
    xt = x.reshape(t, d)
    for l in range(depth):
        if l % 2 == 0:
            e = l // 2
            u = _norm_matmul(xt, attn_norm[l], bf(w_in_ar[e]))
            o_a, o_b = _mix0(u, lb_all[l], hgrn_onorm[e], rwkv_mu[e], rwkv_w0[e], rwkv_w2[e],
                             rwkv_a0[e], rwkv_a2[e], rwkv_g2[e], rwkv_kk[e], rwkv_ka[e],
                             rwkv_rk[e], rwkv_ln_w[e], rwkv_ln_b[e], batch, seq)
            w_o = bf(w_out_ar[e])
            mix, w_o = [o_a, o_b], [w_o[:hw], w_o[hw:]]
        else:
            o = l // 2
            q, k, vt, kmean = _qk_prep(xt, attn_norm[l], bf(w_qkv[o]), q_norm[o], k_norm[o],
                                       batch, seq)
            kmean = kmean.reshape(batch, -1, d)
            mix, w_o = [_moba(q, k, vt, kmean, batch, seq)], [bf(w_o_attn[o])]
        xt = _post_mixer(xt, mix, w_o, mlp_norm[l], bf(w_up[l]), bf(w_down[l]),
                         p[l].reshape(t, -1), bf(ple_proj[l]), ple_norm[l], bf(ple_gate[l]))
    return xt.reshape(batch, seq, d)
```
